```python
import math
import jax
import jax.numpy as jnp
from jax import lax
import numpy as np

D_MODEL = 2048
BATCH = 4
SEQ = 8192
DEPTH = 1
DEC_BATCH = 8
DEC_SEQ = 2048
PAST_LEN = 128

N_HEADS = 32
N_KV_HEADS = 4
HEAD_DIM = 64
ATTN_WIDTH = N_HEADS * HEAD_DIM
KV_WIDTH = N_KV_HEADS * HEAD_DIM
WINDOW = 128
ATTN_BLOCK = 128
N_BUCKETS = 32
MAX_DISTANCE = 128
N_FOURIER_GROUPS = 8
FOURIER_GROUP = 128
FOURIER_WIDTH = N_FOURIER_GROUPS * FOURIER_GROUP
IN_WIDTH = ATTN_WIDTH + 2 * KV_WIDTH + FOURIER_WIDTH + 2 * D_MODEL
N_EXPERTS = 256
TOP_K = 8
N_EXPERT_GROUPS = 8
TOPK_GROUPS = 4
D_EXPERT = D_MODEL // 4
D_SHARED = D_MODEL // 4
ROUTED_SCALE = 2.5
MOE_BLOCK = 128
EPS = 1e-6
NEG_INF = -1e30

kernel_name = 'hybrid_bidir_swa_fnet_moe_adaln'


def rms_norm(x, g):
    xf = x.astype(jnp.float32)
    y = xf * lax.rsqrt(jnp.mean(xf * xf, axis=-1, keepdims=True) + EPS)
    return (y * g.astype(jnp.float32)).astype(x.dtype)


def modulate(x, g, shift, scale):
    return rms_norm(x, g) * (1.0 + scale[:, None, :]) + shift[:, None, :]


def t5_buckets(rel):
    nb = N_BUCKETS // 2
    max_exact = nb // 2
    ret = jnp.where(rel > 0, nb, 0)
    n = jnp.abs(rel)
    nf = jnp.maximum(n, 1).astype(jnp.float32)
    large = max_exact + (jnp.log(nf / max_exact) / math.log(MAX_DISTANCE / max_exact)
                         * (nb - max_exact)).astype(jnp.int32)
    large = jnp.minimum(large, nb - 1)
    return ret + jnp.where(n < max_exact, n, large)


def band_attention(q, k, v, rel_bias, sink):
    B, S = q.shape[0], q.shape[1]
    G = N_HEADS // N_KV_HEADS
    nb = S // ATTN_BLOCK
    span = ATTN_BLOCK + 2 * WINDOW
    q = q.reshape(B, S, N_KV_HEADS, G, HEAD_DIM)
    pad = ((0, 0), (WINDOW, WINDOW), (0, 0), (0, 0))
    kp = jnp.pad(k, pad)
    vp = jnp.pad(v, pad)
    qi = jnp.arange(ATTN_BLOCK)[:, None]
    kj = jnp.arange(span)[None, :]
    rel = kj - WINDOW - qi
    band = jnp.abs(rel) <= WINDOW
    bias = rel_bias[t5_buckets(rel)].astype(jnp.float32)
    bias = jnp.transpose(bias, (2, 0, 1)).reshape(N_KV_HEADS, G, ATTN_BLOCK, span)
    sink_l = sink.astype(jnp.float32).reshape(N_KV_HEADS, G, 1, 1)
    scale = HEAD_DIM ** -0.5

    def block(i):
        s0 = i * ATTN_BLOCK
        qb = lax.dynamic_slice_in_dim(q, s0, ATTN_BLOCK, axis=1)
        kb = lax.dynamic_slice_in_dim(kp, s0, span, axis=1)
        vb = lax.dynamic_slice_in_dim(vp, s0, span, axis=1)
        logits = jnp.einsum('bqkgd,bjkd->bkgqj', qb, kb).astype(jnp.float32) * scale + bias
        kpos = s0 - WINDOW + jnp.arange(span)
        valid = band & ((kpos >= 0) & (kpos < S))[None, :]
        logits = jnp.where(valid, logits, NEG_INF)
        m = jnp.maximum(jnp.max(logits, axis=-1, keepdims=True), sink_l)
        p = jnp.exp(logits - m)
        denom = jnp.sum(p, axis=-1, keepdims=True) + jnp.exp(sink_l - m)
        return jnp.einsum('bkgqj,bjkd->bqkgd', (p / denom).astype(vb.dtype), vb)

    o = lax.map(block, jnp.arange(nb))
    return jnp.moveaxis(o, 0, 1).reshape(B, S, ATTN_WIDTH)


def fourier_mix(u):
    B, S = u.shape[0], u.shape[1]
    uf = u.astype(jnp.float32).reshape(B, S, N_FOURIER_GROUPS, FOURIER_GROUP)
    f = jnp.fft.fft2(uf, axes=(1, 3), norm='ortho').real
    return f.reshape(B, S, FOURIER_WIDTH).astype(u.dtype)


def swiglu(h, w_gate, w_up, w_down):
    return (jax.nn.silu(h @ w_gate) * (h @ w_up)) @ w_down


def route(h, w_router, router_bias):
    T = h.shape[0]
    scores = jax.nn.sigmoid(jnp.dot(h.astype(jnp.float32), w_router.astype(jnp.float32)))
    sel = scores + router_bias.astype(jnp.float32)
    grp = sel.reshape(T, N_EXPERT_GROUPS, N_EXPERTS // N_EXPERT_GROUPS)
    grp_score = jnp.sum(lax.top_k(grp, 2)[0], axis=-1)
    _, gidx = lax.top_k(grp_score, TOPK_GROUPS)
    gmask = jnp.sum(jax.nn.one_hot(gidx, N_EXPERT_GROUPS, dtype=jnp.float32), axis=1) > 0
    emask = jnp.repeat(gmask, N_EXPERTS // N_EXPERT_GROUPS, axis=1)
    sel = jnp.where(emask, sel, -jnp.inf)
    _, idx = lax.top_k(sel, TOP_K)
    w = jnp.take_along_axis(scores, idx, axis=1)
    w = w / jnp.sum(w, axis=-1, keepdims=True) * ROUTED_SCALE
    return idx, w


def routed_experts(h, idx, w, w_gate, w_up, w_down):
    T, D = h.shape
    A = T * TOP_K
    e_flat = idx.reshape(-1).astype(jnp.int32)
    tok_flat = jnp.repeat(jnp.arange(T, dtype=jnp.int32), TOP_K)
    w_flat = w.reshape(-1)
    order = jnp.argsort(e_flat)
    e_s = e_flat[order]
    counts = jnp.bincount(e_flat, length=N_EXPERTS).astype(jnp.int32)
    starts = jnp.cumsum(counts) - counts
    rank = jnp.arange(A, dtype=jnp.int32) - starts[e_s]
    padded = (counts + MOE_BLOCK - 1) // MOE_BLOCK * MOE_BLOCK
    pend = jnp.cumsum(padded)
    pstart = pend - padded
    slot = pstart[e_s] + rank
    n_blocks = -(-A // MOE_BLOCK) + N_EXPERTS
    n_slots = n_blocks * MOE_BLOCK
    slot_tok = jnp.zeros((n_slots,), jnp.int32).at[slot].set(tok_flat[order])
    slot_w = jnp.zeros((n_slots,), jnp.float32).at[slot].set(w_flat[order])
    block_start = jnp.arange(n_blocks, dtype=jnp.int32) * MOE_BLOCK
    block_e = jnp.minimum(jnp.searchsorted(pend, block_start, side='right'), N_EXPERTS - 1)

    def body(acc, xs):
        e, tok, wt = xs
        xb = h[tok]
        y = swiglu(xb, w_gate[e], w_up[e], w_down[e])
        return acc.at[tok].add(y.astype(jnp.float32) * wt[:, None]), None

    acc, _ = lax.scan(body, jnp.zeros((T, D), jnp.float32),
                      (block_e, slot_tok.reshape(n_blocks, MOE_BLOCK), slot_w.reshape(n_blocks, MOE_BLOCK)))
    return acc.astype(h.dtype)


def encoder(x, c, w_ada, b_ada, norm_mix, w_in, rel_bias, sink, w_attn_proj, w_four, b_four, w_out,
            norm_ffn, w_router, router_bias, w_gate_e, w_up_e, w_down_e, w_gate_s, w_up_s, w_down_s,
            norm_final):
    B, S, D = x.shape
    splits = [ATTN_WIDTH, ATTN_WIDTH + KV_WIDTH, ATTN_WIDTH + 2 * KV_WIDTH,
              ATTN_WIDTH + 2 * KV_WIDTH + FOURIER_WIDTH, ATTN_WIDTH + 2 * KV_WIDTH + FOURIER_WIDTH + D_MODEL]
    for l in range(DEPTH):
        ada = jax.nn.silu(c) @ w_ada[l] + b_ada[l]
        sh_m, sc_m, g_m, sh_f, sc_f, g_f = jnp.split(ada, 6, axis=-1)
        h = modulate(x, norm_mix[l], sh_m, sc_m)
        z = h @ w_in[l]
        q, k, v, u, ga, gf = jnp.split(z, splits, axis=-1)
        q = q.reshape(B, S, N_HEADS, HEAD_DIM)
        k = k.reshape(B, S, N_KV_HEADS, HEAD_DIM)
        v = v.reshape(B, S, N_KV_HEADS, HEAD_DIM)
        a = band_attention(q, k, v, rel_bias, sink[l]) @ w_attn_proj[l]
        f = fourier_mix(u) @ w_four[l] + b_four[l]
        mixed = jax.nn.sigmoid(ga) * a + jax.nn.sigmoid(gf) * f
        x = x + g_m[:, None, :] * (mixed @ w_out[l])
        h = modulate(x, norm_ffn[l], sh_f, sc_f).reshape(B * S, D)
        idx, wts = route(h, w_router[l], router_bias[l])
        y = swiglu(h, w_gate_s[l], w_up_s[l], w_down_s[l]) + routed_experts(
            h, idx, wts, w_gate_e[l], w_up_e[l], w_down_e[l])
        x = x + g_f[:, None, :] * y.reshape(B, S, D)
    return rms_norm(x, norm_final)


def setup_inputs(seed: int = 0) -> dict:
    key = jax.random.key(seed)
    ks = jax.random.split(key, 26)
    f32 = jnp.float32
    D = D_MODEL

    def nrm(k, shape, s):
        return jax.random.normal(k, shape, f32) * s

    return {
        'x_prompt': nrm(ks[0], (BATCH, SEQ, D), 1.0),
        'x_sample': nrm(ks[1], (DEC_BATCH, DEC_SEQ, D), 1.0),
        'c_prompt': nrm(ks[2], (BATCH, D), 1.0),
        'c_sample': nrm(ks[3], (DEC_BATCH, D), 1.0),
        'w_ada': nrm(ks[4], (DEPTH, D, 6 * D), 0.5 * D ** -0.5),
        'b_ada': nrm(ks[5], (DEPTH, 6 * D), 0.01),
        'norm_mix': 1.0 + nrm(ks[6], (DEPTH, D), 0.02),
        'w_in': nrm(ks[7], (DEPTH, D, IN_WIDTH), D ** -0.5),
        'rel_bias': nrm(ks[8], (N_BUCKETS, N_HEADS), 0.5),
        'sink': nrm(ks[9], (DEPTH, N_HEADS), 1.0),
        'w_attn_proj': nrm(ks[10], (DEPTH, ATTN_WIDTH, D), ATTN_WIDTH ** -0.5),
        'w_four': nrm(ks[11], (DEPTH, FOURIER_WIDTH, D), FOURIER_WIDTH ** -0.5),
        'b_four': nrm(ks[12], (DEPTH, D), 0.01),
        'w_out': nrm(ks[13], (DEPTH, D, D), D ** -0.5),
        'norm_ffn': 1.0 + nrm(ks[14], (DEPTH, D), 0.02),
        'w_router': nrm(ks[15], (DEPTH, D, N_EXPERTS), D ** -0.5),
        'router_bias': nrm(ks[16], (DEPTH, N_EXPERTS), 0.01),
        'w_gate_e': nrm(ks[17], (DEPTH, N_EXPERTS, D, D_EXPERT), D ** -0.5),
        'w_up_e': nrm(ks[18], (DEPTH, N_EXPERTS, D, D_EXPERT), D ** -0.5),
        'w_down_e': nrm(ks[19], (DEPTH, N_EXPERTS, D_EXPERT, D), D_EXPERT ** -0.5),
        'w_gate_s': nrm(ks[20], (DEPTH, D, D_SHARED), D ** -0.5),
        'w_up_s': nrm(ks[21], (DEPTH, D, D_SHARED), D ** -0.5),
        'w_down_s': nrm(ks[22], (DEPTH, D_SHARED, D), D_SHARED ** -0.5),
        'norm_final': 1.0 + nrm(ks[23], (D,), 0.02),
    }


def reference(x_prompt, x_sample, c_prompt, c_sample, w_ada, b_ada, norm_mix, w_in, rel_bias, sink,
              w_attn_proj, w_four, b_four, w_out, norm_ffn, w_router, router_bias, w_gate_e, w_up_e,
              w_down_e, w_gate_s, w_up_s, w_down_s, norm_final):
    y_prompt = encoder(x_prompt, c_prompt, w_ada, b_ada, norm_mix, w_in, rel_bias, sink, w_attn_proj,
                       w_four, b_four, w_out, norm_ffn, w_router, router_bias, w_gate_e, w_up_e, w_down_e,
                       w_gate_s, w_up_s, w_down_s, norm_final)
    y_sample = encoder(x_sample, c_sample, w_ada, b_ada, norm_mix, w_in, rel_bias, sink, w_attn_proj,
                       w_four, b_four, w_out, norm_ffn, w_router, router_bias, w_gate_e, w_up_e, w_down_e,
                       w_gate_s, w_up_s, w_down_s, norm_final)
    return (y_prompt, y_sample)
```

```python
import functools
import math
from typing import NamedTuple

import jax
import jax.numpy as jnp
import numpy as np
from jax import lax
from jax.experimental import pallas as pl
from jax.experimental.pallas import tpu as pltpu

F32 = jnp.float32
BF16 = jnp.bfloat16
I32 = jnp.int32
U32 = jnp.uint32

EPS = 1e-6
NEG_INF = -1e30
LANES = 128
MIB = 1024 * 1024


class Cfg(NamedTuple):
    d_model: int = 2048
    batch: int = 4
    seq: int = 8192
    dec_batch: int = 8
    dec_seq: int = 2048
    n_heads: int = 32
    n_kv: int = 4
    head_dim: int = 64
    window: int = 128
    n_buckets: int = 32
    max_distance: int = 128
    n_fgroups: int = 8
    fgroup: int = 128
    n_experts: int = 256
    top_k: int = 8
    n_egroups: int = 8
    topk_groups: int = 4
    d_expert: int = 512
    d_shared: int = 512
    routed_scale: float = 2.5
    tm_norm: int = 512
    tm_mm: int = 512
    tm_proj: int = 256
    tm_router: int = 256
    tm_dispatch: int = 256
    tm_expert: int = 256
    tm_final: int = 256
    four_n2: int = 128
    four_t2: int = 8

    @property
    def attn_width(self):
        return self.n_heads * self.head_dim

    @property
    def kv_width(self):
        return self.n_kv * self.head_dim

    @property
    def fourier_width(self):
        return self.n_fgroups * self.fgroup

    @property
    def tokens_p(self):
        return self.batch * self.seq

    @property
    def tokens_s(self):
        return self.dec_batch * self.dec_seq

    @property
    def tokens(self):
        return self.tokens_p + self.tokens_s


def _params(vmem_mib):
    return pltpu.CompilerParams(vmem_limit_bytes=int(vmem_mib * MIB))


def _batch_of_tile(i, cfg, tm):
    n_p = cfg.tokens_p // tm
    return jnp.where(i < n_p, i // (cfg.seq // tm), cfg.batch + (i - n_p) // (cfg.dec_seq // tm))


def _ada_spec(cfg, tm, which):
    return pl.BlockSpec((None, 1, cfg.d_model), lambda i: (_batch_of_tile(i, cfg, tm) * 6 + which, 0, 0))


def _two_group_specs(cfg, tm, width):
    n_p = cfg.tokens_p // tm
    return [pl.BlockSpec((tm, width), lambda i: (jnp.minimum(i, n_p - 1), 0)),
            pl.BlockSpec((tm, width), lambda i: (jnp.maximum(i - n_p, 0), 0))]


def _modulated_norm(x, g, shift, scale):
    y = x * lax.rsqrt(jnp.mean(x * x, axis=-1, keepdims=True) + EPS) * g
    return y * (1.0 + scale) + shift


def _pack_bf16_pair(lo, hi):
    lo_bits = pltpu.bitcast(lo.astype(BF16).astype(F32), U32) >> 16
    hi_bits = pltpu.bitcast(hi.astype(BF16).astype(F32), U32) & jnp.uint32(0xFFFF0000)
    return hi_bits | lo_bits


def _unpack_bf16_pair(p):
    lo = pltpu.bitcast(p << 16, F32)
    hi = pltpu.bitcast(p & jnp.uint32(0xFFFF0000), F32)
    return lo, hi


def _ada_kernel(c_ref, w_ref, b_ref, o_ref):
    c = c_ref[...]
    s = c * jax.nn.sigmoid(c)
    o_ref[...] = jnp.dot(s, w_ref[...], preferred_element_type=F32, precision=lax.Precision.HIGHEST) + b_ref[...]


def _ada(cfg, c_all, w_ada, b_ada):
    rows, d = c_all.shape
    n = w_ada.shape[1]
    tn = n // 8
    return pl.pallas_call(
        _ada_kernel, grid=(n // tn,),
        in_specs=[pl.BlockSpec((rows, d), lambda j: (0, 0)), pl.BlockSpec((d, tn), lambda j: (0, j)),
                  pl.BlockSpec((1, tn), lambda j: (0, j))],
        out_specs=pl.BlockSpec((rows, tn), lambda j: (0, j)),
        out_shape=jax.ShapeDtypeStruct((rows, n), F32),
        compiler_params=_params(2 * d * tn * 4 / MIB + 8), name="ada")(c_all, w_ada, b_ada)


def _modnorm_kernel(xp_ref, xs_ref, g_ref, sh_ref, sc_ref, o_ref, *, n_p):
    i = pl.program_id(0)

    def run(x_ref):
        o_ref[...] = _modulated_norm(x_ref[...], g_ref[...], sh_ref[...], sc_ref[...]).astype(o_ref.dtype)

    pl.when(i < n_p)(lambda: run(xp_ref))
    pl.when(i >= n_p)(lambda: run(xs_ref))


def _modnorm(cfg, x_p, x_s, g, ada3):
    tm, d = cfg.tm_norm, cfg.d_model
    return pl.pallas_call(
        functools.partial(_modnorm_kernel, n_p=cfg.tokens_p // tm), grid=(cfg.tokens // tm,),
        in_specs=_two_group_specs(cfg, tm, d) + [pl.BlockSpec((1, d), lambda i: (0, 0)),
                                                 _ada_spec(cfg, tm, 0), _ada_spec(cfg, tm, 1)],
        out_specs=pl.BlockSpec((tm, d), lambda i: (i, 0)),
        out_shape=jax.ShapeDtypeStruct((cfg.tokens, d), BF16),
        compiler_params=_params(7 * tm * d * 4 / MIB + 8), name="modnorm")(x_p, x_s, g, ada3, ada3)


def _mm_kernel(a_ref, w_ref, o_ref):
    o_ref[...] = jnp.dot(a_ref[...], w_ref[...], preferred_element_type=F32).astype(o_ref.dtype)


def _matmul(a, w, tm, tn, name):
    m, k = a.shape
    n = w.shape[1]
    vmem = (2 * tm * k * 2 + 2 * k * tn * 2 + 2 * tm * tn * 2 + tm * tn * 4) / MIB + 8
    return pl.pallas_call(
        _mm_kernel, grid=(n // tn, m // tm),
        in_specs=[pl.BlockSpec((tm, k), lambda j, i: (i, 0)), pl.BlockSpec((k, tn), lambda j, i: (0, j))],
        out_specs=pl.BlockSpec((tm, tn), lambda j, i: (i, j)),
        out_shape=jax.ShapeDtypeStruct((m, n), BF16),
        compiler_params=_params(vmem), name=name)(a, w)


def _t5_bucket_table(cfg):
    blk, span = cfg.window, 3 * cfg.window
    rel = (np.arange(span)[None, :] - cfg.window - np.arange(blk)[:, None]).astype(np.int32)
    nb = cfg.n_buckets // 2
    max_exact = nb // 2
    ret = np.where(rel > 0, nb, 0)
    n = np.abs(rel)
    nf = np.maximum(n, 1).astype(np.float32)
    ratio = np.log(nf / np.float32(max_exact)) / np.float32(math.log(cfg.max_distance / max_exact))
    large = max_exact + (ratio * np.float32(nb - max_exact)).astype(np.int32)
    large = np.minimum(large, nb - 1)
    return (ret + np.where(n < max_exact, n, large)).astype(np.int32), rel


def _bias_kernel(bucket_ref, rbt_ref, o_ref, *, n_buckets):
    bucket = bucket_ref[...]
    acc = jnp.zeros(o_ref.shape, F32)
    for b in range(n_buckets):
        acc = jnp.where(bucket == b, rbt_ref[:, b:b + 1], acc)
    o_ref[...] = acc


def _bias_table(cfg, rel_bias):
    buckets, _ = _t5_bucket_table(cfg)
    blk, span, h = cfg.window, 3 * cfg.window, cfg.n_heads
    n = blk * span
    flat = pl.pallas_call(
        functools.partial(_bias_kernel, n_buckets=cfg.n_buckets),
        out_shape=jax.ShapeDtypeStruct((h, n), F32), name="bias")(
            jnp.asarray(buckets.reshape(1, n)), rel_bias.T.astype(F32))
    pairs = h // cfg.n_kv // 2
    t = flat.reshape(cfg.n_kv, pairs, 2, blk, span)
    return jnp.transpose(t, (0, 1, 3, 2, 4)).reshape(cfg.n_kv, pairs * blk, 2 * span)


def _attn_kernel(q_ref, kp_ref, kc_ref, kn_ref, vp_ref, vc_ref, vn_ref, bias_ref, sink_ref, rel_ref, kofs_ref,
                 o_ref, valid_ref, *, cfg, blocks_p):
    blk, hd = cfg.window, cfg.head_dim
    span = 3 * blk
    pairs = cfg.n_heads // cfg.n_kv // 2
    nb_p, nb_s = cfg.seq // blk, cfg.dec_seq // blk
    i = pl.program_id(0)
    in_p = i < blocks_p
    pos = jnp.where(in_p, i % nb_p, (i - blocks_p) % nb_s)
    seq_len = jnp.where(in_p, cfg.seq, cfg.dec_seq)
    kpos = pos * blk + kofs_ref[...]
    in_band = jnp.abs(rel_ref[...]) <= cfg.window
    in_seq = (kpos >= 0) & (kpos < seq_len)
    valid_ref[...] = jnp.where(in_band, jnp.where(in_seq, 1.0, 0.0), 0.0)

    k = jnp.concatenate([kp_ref[...], kc_ref[...], kn_ref[...]], axis=0)
    v = jnp.concatenate([vp_ref[...], vc_ref[...], vn_ref[...]], axis=0)
    scale = hd ** -0.5
    lane = lax.broadcasted_iota(I32, (pairs * blk, 2 * hd), 1)
    for j in range(cfg.n_kv):
        kj = k[:, j * hd:(j + 1) * hd]
        vj = v[:, j * hd:(j + 1) * hd]
        zero = jnp.zeros_like(kj)
        k2 = jnp.concatenate([jnp.concatenate([kj, zero], axis=1), jnp.concatenate([zero, kj], axis=1)], axis=0)
        v2 = jnp.concatenate([jnp.concatenate([vj, zero], axis=1), jnp.concatenate([zero, vj], axis=1)], axis=0)
        base = j * pairs * 2 * hd
        qs = jnp.concatenate([q_ref[:, base + p * 2 * hd: base + (p + 1) * 2 * hd] for p in range(pairs)], axis=0)
        logits = lax.dot_general(qs, k2, (((1,), (1,)), ((), ())), preferred_element_type=F32)
        logits = logits * scale + bias_ref[j]
        logits = jnp.where(valid_ref[...] > 0.5, logits, NEG_INF)
        sink = sink_ref[j]
        probs, inv = [], []
        for half in range(2):
            lh = logits[:, half * span:(half + 1) * span]
            sk = sink[:, half:half + 1]
            m = jnp.maximum(jnp.max(lh, axis=-1, keepdims=True), sk)
            p = jnp.exp(lh - m)
            denom = jnp.sum(p, axis=-1, keepdims=True) + jnp.exp(sk - m)
            probs.append(p.astype(BF16))
            inv.append(1.0 / denom)
        pv = jnp.dot(jnp.concatenate(probs, axis=1), v2, preferred_element_type=F32)
        out = pv * jnp.where(lane < hd, inv[0], inv[1])
        for p in range(pairs):
            o_ref[:, base + p * 2 * hd: base + (p + 1) * 2 * hd] = out[p * blk:(p + 1) * blk].astype(o_ref.dtype)


def _attention(cfg, qkv, bias_tbl, sink):
    blk, hd = cfg.window, cfg.head_dim
    aw, kvw = cfg.attn_width, cfg.kv_width
    pairs = cfg.n_heads // cfg.n_kv // 2
    nb_p, nb_s = cfg.seq // blk, cfg.dec_seq // blk
    blocks_p = cfg.batch * nb_p
    n_blocks = cfg.tokens // blk
    _, rel = _t5_bucket_table(cfg)
    rel_tbl = np.tile(rel, (pairs, 2)).astype(np.int32)
    kofs_tbl = np.tile((np.arange(3 * blk) - cfg.window)[None, :], (pairs * blk, 2)).astype(np.int32)
    sink_tbl = jnp.broadcast_to(sink.astype(F32).reshape(cfg.n_kv, pairs, 1, 2),
                                (cfg.n_kv, pairs, blk, 2)).reshape(cfg.n_kv, pairs * blk, 2)

    def seq_pos(i):
        in_p = i < blocks_p
        return jnp.where(in_p, i % nb_p, (i - blocks_p) % nb_s), jnp.where(in_p, nb_p, nb_s)

    def prev_blk(i):
        pos, _ = seq_pos(i)
        return jnp.where(pos == 0, i, i - 1)

    def next_blk(i):
        pos, nb = seq_pos(i)
        return jnp.where(pos == nb - 1, i, i + 1)

    kcol, vcol = aw // kvw, aw // kvw + 1
    full = lambda shape: pl.BlockSpec(shape, lambda i: (0,) * len(shape))
    return pl.pallas_call(
        functools.partial(_attn_kernel, cfg=cfg, blocks_p=blocks_p), grid=(n_blocks,),
        in_specs=[pl.BlockSpec((blk, aw), lambda i: (i, 0)),
                  pl.BlockSpec((blk, kvw), lambda i: (prev_blk(i), kcol)),
                  pl.BlockSpec((blk, kvw), lambda i: (i, kcol)),
                  pl.BlockSpec((blk, kvw), lambda i: (next_blk(i), kcol)),
                  pl.BlockSpec((blk, kvw), lambda i: (prev_blk(i), vcol)),
                  pl.BlockSpec((blk, kvw), lambda i: (i, vcol)),
                  pl.BlockSpec((blk, kvw), lambda i: (next_blk(i), vcol)),
                  full(bias_tbl.shape), full(sink_tbl.shape), full(rel_tbl.shape), full(kofs_tbl.shape)],
        out_specs=pl.BlockSpec((blk, aw), lambda i: (i, 0)),
        out_shape=jax.ShapeDtypeStruct((cfg.tokens, aw), BF16),
        scratch_shapes=[pltpu.VMEM(rel_tbl.shape, F32)],
        compiler_params=_params(40), name="attn")(
            qkv, qkv, qkv, qkv, qkv, qkv, qkv, bias_tbl, sink_tbl, jnp.asarray(rel_tbl), jnp.asarray(kofs_tbl))


def _dft_cos_sin(n):
    ang = 2.0 * np.pi * np.outer(np.arange(n), np.arange(n)) / n
    return np.cos(ang), np.sin(ang)


def _four1_kernel(u_ref, m1_ref, twr_ref, twi_ref, zr_ref, zi_ref, *, n1, t2, width):
    y = jnp.dot(m1_ref[...], u_ref[...], preferred_element_type=F32)
    twr, twi = twr_ref[0], twi_ref[0]
    for l in range(t2):
        sl = slice(l * width, (l + 1) * width)
        yr, yi = y[:n1, sl], y[n1:, sl]
        cr, ci = twr[:, l:l + 1], twi[:, l:l + 1]
        zr_ref[:, sl] = (yr * cr - yi * ci).astype(zr_ref.dtype)
        zi_ref[:, sl] = (yr * ci + yi * cr).astype(zi_ref.dtype)


def _four3_kernel(zr_ref, zi_ref, m3_ref, mc_ref, o_ref, *, n2, n_groups, group, norm):
    z = jnp.concatenate([zr_ref[0], zi_ref[0]], axis=0)
    x = jnp.dot(m3_ref[...], z, preferred_element_type=F32)
    xr, xi = x[:n2].astype(BF16), x[n2:].astype(BF16)
    for g in range(n_groups):
        sl = slice(g * group, (g + 1) * group)
        xg = jnp.concatenate([xr[:, sl], xi[:, sl]], axis=1)
        o_ref[:, sl] = (jnp.dot(xg, mc_ref[...], preferred_element_type=F32) * norm).astype(o_ref.dtype)


def _fourier_group(cfg, u2, row_off, nbatch, seq):
    n2, t2, width = cfg.four_n2, cfg.four_t2, cfg.fourier_width
    n1 = seq // n2
    c1, s1 = _dft_cos_sin(n1)
    m1 = jnp.asarray(np.concatenate([c1, -s1], axis=0), BF16)
    ang = 2.0 * np.pi * np.outer(np.arange(n1), np.arange(n2)) / seq
    twr = jnp.asarray(np.cos(ang).reshape(n1, n2 // t2, t2).transpose(1, 0, 2), F32)
    twi = jnp.asarray((-np.sin(ang)).reshape(n1, n2 // t2, t2).transpose(1, 0, 2), F32)
    blk_off = row_off // n1
    zshape = jax.ShapeDtypeStruct((nbatch * n1, n2 * width), BF16)
    tw_spec = pl.BlockSpec((1, n1, t2), lambda b, j: (j, 0, 0))
    zr, zi = pl.pallas_call(
        functools.partial(_four1_kernel, n1=n1, t2=t2, width=width), grid=(nbatch, n2 // t2),
        in_specs=[pl.BlockSpec((n1, t2 * width), lambda b, j: (blk_off + b, j)),
                  pl.BlockSpec((2 * n1, n1), lambda b, j: (0, 0)), tw_spec, tw_spec],
        out_specs=[pl.BlockSpec((n1, t2 * width), lambda b, j: (b, j))] * 2,
        out_shape=[zshape, zshape],
        compiler_params=_params(32), name="four1")(u2, m1, twr, twi)

    c2, s2 = _dft_cos_sin(n2)
    m3 = jnp.asarray(np.block([[c2, s2], [-s2, c2]]), BF16)
    cc, sc = _dft_cos_sin(cfg.fgroup)
    mc = jnp.asarray(np.concatenate([cc, sc], axis=0), BF16)
    norm = 1.0 / math.sqrt(seq * cfg.fgroup)
    z_spec = pl.BlockSpec((1, n2, width), lambda b, k1: (b * n1 + k1, 0, 0))
    f = pl.pallas_call(
        functools.partial(_four3_kernel, n2=n2, n_groups=cfg.n_fgroups, group=cfg.fgroup, norm=norm),
        grid=(nbatch, n1),
        in_specs=[z_spec, z_spec, pl.BlockSpec((2 * n2, 2 * n2), lambda b, k1: (0, 0)),
                  pl.BlockSpec((2 * cfg.fgroup, cfg.fgroup), lambda b, k1: (0, 0))],
        out_specs=pl.BlockSpec((n2, width), lambda b, k1: (b, k1)),
        out_shape=jax.ShapeDtypeStruct((nbatch * n2, n1 * width), BF16),
        compiler_params=_params(32), name="four3")(
            zr.reshape(nbatch * n1, n2, width), zi.reshape(nbatch * n1, n2, width), m3, mc)
    return f.reshape(nbatch * seq, width)


def _proj_kernel(a_ref, fp_ref, fs_ref, g_ref, wa_ref, wf_ref, bf_ref, o_ref, *, n_p, d):
    i = pl.program_id(0)
    a = jnp.dot(a_ref[...], wa_ref[...], preferred_element_type=F32)

    def run(f_ref):
        f = jnp.dot(f_ref[...], wf_ref[...], preferred_element_type=F32) + bf_ref[...]
        ga = g_ref[:, :d].astype(F32)
        gf = g_ref[:, d:].astype(F32)
        o_ref[...] = (jax.nn.sigmoid(ga) * a + jax.nn.sigmoid(gf) * f).astype(o_ref.dtype)

    pl.when(i < n_p)(lambda: run(fp_ref))
    pl.when(i >= n_p)(lambda: run(fs_ref))


def _proj(cfg, attn, f_p, f_s, gates, wa, wf, bf):
    tm, d, aw, fw = cfg.tm_proj, cfg.d_model, cfg.attn_width, cfg.fourier_width
    const = lambda shape: pl.BlockSpec(shape, lambda i: (0, 0))
    vmem = (2 * (aw + fw) * d * 2 + 2 * tm * (aw + 2 * fw + 3 * d) * 2 + 3 * tm * d * 4) / MIB + 8
    return pl.pallas_call(
        functools.partial(_proj_kernel, n_p=cfg.tokens_p // tm, d=d), grid=(cfg.tokens // tm,),
        in_specs=[pl.BlockSpec((tm, aw), lambda i: (i, 0))] + _two_group_specs(cfg, tm, fw) + [
            pl.BlockSpec((tm, 2 * d), lambda i: (i, 0)), const((aw, d)), const((fw, d)), const((1, d))],
        out_specs=pl.BlockSpec((tm, d), lambda i: (i, 0)),
        out_shape=jax.ShapeDtypeStruct((cfg.tokens, d), BF16),
        compiler_params=_params(vmem), name="proj")(attn, f_p, f_s, gates, wa, wf, bf)


def _resid_kernel(m_ref, wo_ref, xp_ref, xs_ref, gate_ref, o_ref, *, n_p):
    i = pl.program_id(0)
    out = jnp.dot(m_ref[...], wo_ref[...], preferred_element_type=F32)

    def run(x_ref):
        o_ref[...] = x_ref[...] + gate_ref[...] * out

    pl.when(i < n_p)(lambda: run(xp_ref))
    pl.when(i >= n_p)(lambda: run(xs_ref))


def _resid(cfg, mixed, wo, x_p, x_s, ada3):
    tm, d = cfg.tm_proj, cfg.d_model
    vmem = (2 * d * d * 2 + 2 * tm * d * 2 + 7 * tm * d * 4) / MIB + 8
    return pl.pallas_call(
        functools.partial(_resid_kernel, n_p=cfg.tokens_p // tm), grid=(cfg.tokens // tm,),
        in_specs=[pl.BlockSpec((tm, d), lambda i: (i, 0)), pl.BlockSpec((d, d), lambda i: (0, 0))]
        + _two_group_specs(cfg, tm, d) + [_ada_spec(cfg, tm, 2)],
        out_specs=pl.BlockSpec((tm, d), lambda i: (i, 0)),
        out_shape=jax.ShapeDtypeStruct((cfg.tokens, d), F32),
        compiler_params=_params(vmem), name="resid")(mixed, wo, x_p, x_s, ada3)


def _router_kernel(x_ref, g_ref, sh_ref, sc_ref, wrt_ref, rb_ref, tri_ref,
                   hp_ref, idx_ref, wgt_ref, pos_ref, cnt_ref, carry_ref, *, cfg):
    d, ne, ng = cfg.d_model, cfg.n_experts, cfg.n_egroups
    per = ne // ng
    i = pl.program_id(0)

    @pl.when(i == 0)
    def _():
        carry_ref[...] = jnp.zeros_like(carry_ref)

    h = _modulated_norm(x_ref[...], g_ref[...], sh_ref[...], sc_ref[...])
    tm = h.shape[0]
    hp_ref[...] = _pack_bf16_pair(h[:, :d // 2], h[:, d // 2:]).reshape(tm, 1, d // 2)

    logits = lax.dot_general(wrt_ref[...], h, (((1,), (1,)), ((), ())), preferred_element_type=F32,
                             precision=lax.Precision.HIGHEST)
    scores = jax.nn.sigmoid(logits)
    sel = scores + rb_ref[...]
    s3 = sel.reshape(ng, per, tm)
    io3 = lax.broadcasted_iota(I32, (ng, per, tm), 1)
    m1 = jnp.max(s3, axis=1, keepdims=True)
    i1 = jnp.min(jnp.where(s3 == m1, io3, per), axis=1, keepdims=True)
    m2 = jnp.max(jnp.where(io3 == i1, -jnp.inf, s3), axis=1, keepdims=True)
    gs = (m1 + m2).reshape(ng, tm)
    gio = lax.broadcasted_iota(I32, (ng, tm), 0)
    before = jnp.zeros((ng, tm), I32)
    for g2 in range(ng):
        row = gs[g2:g2 + 1, :]
        ahead = jnp.where(row > gs, 1, jnp.where(row == gs, jnp.where(gio > g2, 1, 0), 0))
        before = before + ahead
    keep = jnp.where(before < cfg.topk_groups, 1.0, 0.0)
    keep_e = jnp.broadcast_to(keep.reshape(ng, 1, tm), (ng, per, tm)).reshape(ne, tm)
    cur = jnp.where(keep_e > 0.5, sel, -jnp.inf)

    eio = lax.broadcasted_iota(I32, (ne, tm), 0)
    chosen, picked = [], jnp.zeros((ne, tm), F32)
    for _ in range(cfg.top_k):
        m = jnp.max(cur, axis=0, keepdims=True)
        ij = jnp.min(jnp.where(cur == m, eio, ne), axis=0, keepdims=True)
        hit = eio == ij
        picked = jnp.where(hit, 1.0, picked)
        cur = jnp.where(hit, -jnp.inf, cur)
        chosen.append(ij)
    before_t = jnp.dot(picked.astype(BF16), tri_ref[...], preferred_element_type=F32)
    rank = carry_ref[...] + before_t
    sc_rows, pos_rows = [], []
    for ij in chosen:
        hit = eio == ij
        sc_rows.append(jnp.sum(jnp.where(hit, scores, 0.0), axis=0, keepdims=True))
        pos_rows.append(jnp.sum(jnp.where(hit, rank, 0.0), axis=0, keepdims=True))
    sc_all = jnp.concatenate(sc_rows, axis=0)
    total = jnp.sum(sc_all, axis=0, keepdims=True)
    idx_ref[...] = jnp.concatenate(chosen, axis=0)
    wgt_ref[...] = sc_all / total * cfg.routed_scale
    pos_ref[...] = jnp.concatenate(pos_rows, axis=0).astype(I32)
    carry_ref[...] = carry_ref[...] + jnp.sum(picked, axis=1, keepdims=True)
    cnt_ref[...] = carry_ref[...].astype(I32)


def _router(cfg, x1, g, ada3, w_router, router_bias):
    tm, d, ne, k = cfg.tm_router, cfg.d_model, cfg.n_experts, cfg.top_k
    t = cfg.tokens
    tri = jnp.asarray(np.triu(np.ones((tm, tm), np.float32), 1), BF16)
    row = lambda dt: jax.ShapeDtypeStruct((k, t), dt)
    row_spec = pl.BlockSpec((k, tm), lambda i: (0, i))
    return pl.pallas_call(
        functools.partial(_router_kernel, cfg=cfg), grid=(t // tm,),
        in_specs=[pl.BlockSpec((tm, d), lambda i: (i, 0)), pl.BlockSpec((1, d), lambda i: (0, 0)),
                  _ada_spec(cfg, tm, 3), _ada_spec(cfg, tm, 4),
                  pl.BlockSpec((ne, d), lambda i: (0, 0)), pl.BlockSpec((ne, 1), lambda i: (0, 0)),
                  pl.BlockSpec((tm, tm), lambda i: (0, 0))],
        out_specs=[pl.BlockSpec((tm, 1, d // 2), lambda i: (i, 0, 0)), row_spec, row_spec, row_spec,
                   pl.BlockSpec((ne, 1), lambda i: (0, 0))],
        out_shape=[jax.ShapeDtypeStruct((t, 1, d // 2), U32), row(I32), row(F32), row(I32),
                   jax.ShapeDtypeStruct((ne, 1), I32)],
        scratch_shapes=[pltpu.VMEM((ne, 1), F32)],
        compiler_params=_params(40), name="router")(
            x1, g, ada3, ada3, w_router.T.astype(F32), router_bias.reshape(ne, 1).astype(F32), tri)


def _dispatch_kernel(pend_ref, padded_ref, nt_ref, hp_ref, slot_ref, xs_ref, zero_ref, sem, zsem, *, cfg):
    tm, tme, k, ne = cfg.tm_dispatch, cfg.tm_expert, cfg.top_k, cfg.n_experts
    n_tiles = xs_ref.shape[0] // tme
    i = pl.program_id(0)

    def pad_copy(e):
        return pltpu.make_async_copy(zero_ref, xs_ref.at[pl.ds(pend_ref[e] - tme, tme)], zsem)

    def tail_copy(t):
        return pltpu.make_async_copy(zero_ref, xs_ref.at[pl.ds(t * tme, tme)], zsem)

    @pl.when(i == 0)
    def _():
        zero_ref[...] = jnp.zeros_like(zero_ref)

        def start(e, c):
            pl.when(padded_ref[e] > 0)(lambda: pad_copy(e).start())
            return c

        def wait(e, c):
            pl.when(padded_ref[e] > 0)(lambda: pad_copy(e).wait())
            return c

        lax.fori_loop(0, ne, start, 0)
        lax.fori_loop(nt_ref[0], n_tiles, lambda t, c: (tail_copy(t).start(), c)[1], 0)
        lax.fori_loop(0, ne, wait, 0)
        lax.fori_loop(nt_ref[0], n_tiles, lambda t, c: (tail_copy(t).wait(), c)[1], 0)

    def row(r, c):
        for j in range(k):
            pltpu.make_async_copy(hp_ref.at[r], xs_ref.at[slot_ref[j, r]], sem).start()
        return c

    lax.fori_loop(0, tm, row, 0)
    for _ in range(k):
        pltpu.make_async_copy(hp_ref, xs_ref.at[pl.ds(0, tm)], sem).wait()


def _dispatch(cfg, hp, slot, pend, padded, n_used, n_slots):
    tm, d, k = cfg.tm_dispatch, cfg.d_model, cfg.top_k
    grid_spec = pltpu.PrefetchScalarGridSpec(
        num_scalar_prefetch=3, grid=(cfg.tokens // tm,),
        in_specs=[pl.BlockSpec((tm, 1, d // 2), lambda i, *_: (i, 0, 0)),
                  pl.BlockSpec((k, tm), lambda i, *_: (0, i), memory_space=pltpu.SMEM)],
        out_specs=pl.BlockSpec(memory_space=pl.ANY),
        scratch_shapes=[pltpu.VMEM((cfg.tm_expert, 1, d // 2), U32), pltpu.SemaphoreType.DMA,
                        pltpu.SemaphoreType.DMA])
    return pl.pallas_call(
        functools.partial(_dispatch_kernel, cfg=cfg), grid_spec=grid_spec,
        out_shape=jax.ShapeDtypeStruct((n_slots, 1, d // 2), U32),
        compiler_params=_params(24), name="dispatch")(pend, padded, n_used, hp, slot)


def _expert_kernel(te_ref, nt_ref, xs_ref, wg_ref, wu_ref, wd_ref, ys_ref, wgb, wub, wdb, rows2d):
    i = pl.program_id(0)
    tme, _, half = xs_ref.shape

    @pl.when(i >= nt_ref[0])
    def _():
        ys_ref[...] = jnp.zeros_like(ys_ref)

    @pl.when(i < nt_ref[0])
    def _():
        new_expert = jnp.logical_or(i == 0, te_ref[i] != te_ref[jnp.maximum(i - 1, 0)])

        @pl.when(new_expert)
        def _():
            wgb[...] = wg_ref[...].astype(BF16)
            wub[...] = wu_ref[...].astype(BF16)
            wdb[...] = wd_ref[...].astype(BF16)

        rows2d[...] = xs_ref[...].reshape(tme, half)
        lo, hi = _unpack_bf16_pair(rows2d[...])
        x = jnp.concatenate([lo.astype(BF16), hi.astype(BF16)], axis=1)
        g = jnp.dot(x, wgb[...], preferred_element_type=F32)
        u = jnp.dot(x, wub[...], preferred_element_type=F32)
        a = (g * jax.nn.sigmoid(g) * u).astype(BF16)
        y = jnp.dot(a, wdb[...], preferred_element_type=F32)
        ys_ref[...] = _pack_bf16_pair(y[:, :half], y[:, half:]).reshape(tme, 1, half)


def _experts(cfg, xs, tile_expert, n_used, wg, wu, wd):
    tme, d, de = cfg.tm_expert, cfg.d_model, cfg.d_expert
    n_tiles = xs.shape[0] // tme
    tile = lambda i, te, nt: (jnp.minimum(i, nt[0] - 1), 0, 0)
    wsel = lambda i, te, nt: (te[jnp.minimum(i, nt[0] - 1)], 0, 0)
    grid_spec = pltpu.PrefetchScalarGridSpec(
        num_scalar_prefetch=2, grid=(n_tiles,),
        in_specs=[pl.BlockSpec((tme, 1, d // 2), tile), pl.BlockSpec((None, d, de), wsel),
                  pl.BlockSpec((None, d, de), wsel), pl.BlockSpec((None, de, d), wsel)],
        out_specs=pl.BlockSpec((tme, 1, d // 2), lambda i, te, nt: (i, 0, 0)),
        scratch_shapes=[pltpu.VMEM((d, de), BF16), pltpu.VMEM((d, de), BF16), pltpu.VMEM((de, d), BF16),
                        pltpu.VMEM((tme, d // 2), U32)])
    vmem = (2 * 3 * d * de * 4 + 3 * d * de * 2 + 4 * tme * d * 2 + 6 * tme * d * 4) / MIB + 8
    return pl.pallas_call(
        _expert_kernel, grid_spec=grid_spec, out_shape=jax.ShapeDtypeStruct(xs.shape, U32),
        compiler_params=_params(vmem), name="experts")(tile_expert, n_used, xs, wg, wu, wd)


def _final_kernel(slot_ref, slotn_ref, x_ref, wt_ref, g_ref, sh_ref, sc_ref, gate_ref, wg_ref, wu_ref, wd_ref,
                  nf_ref, ys_ref, o_ref, buf, rows2d, sem, *, cfg):
    tm, k = cfg.tm_final, cfg.top_k
    half = rows2d.shape[1]
    i = pl.program_id(0)
    n = pl.num_programs(0)

    def plane(b, j):
        return pl.ds(pl.multiple_of((b * k + j) * tm, tm), tm)

    def gather(slots, b):
        def row(r, c):
            for j in range(k):
                pltpu.make_async_copy(ys_ref.at[slots[j, r]], buf.at[(b * k + j) * tm + r], sem.at[b]).start()
            return c

        lax.fori_loop(0, tm, row, 0)

    pl.when(i == 0)(lambda: gather(slot_ref, 0))
    pl.when(i + 1 < n)(lambda: gather(slotn_ref, (i + 1) % 2))
    b = i % 2
    for j in range(k):
        pltpu.make_async_copy(ys_ref.at[pl.ds(0, tm)], buf.at[plane(b, j)], sem.at[b]).wait()

    x = x_ref[...]
    h = _modulated_norm(x, g_ref[...], sh_ref[...], sc_ref[...]).astype(BF16)
    g = jnp.dot(h, wg_ref[...], preferred_element_type=F32)
    u = jnp.dot(h, wu_ref[...], preferred_element_type=F32)
    y = jnp.dot((g * jax.nn.sigmoid(g) * u).astype(BF16), wd_ref[...], preferred_element_type=F32)
    for j in range(k):
        rows2d[...] = buf[plane(b, j)].reshape(tm, half)
        lo, hi = _unpack_bf16_pair(rows2d[...])
        y = y + wt_ref[:, j:j + 1] * jnp.concatenate([lo, hi], axis=1)
    xo = x + gate_ref[...] * y
    o_ref[...] = xo * lax.rsqrt(jnp.mean(xo * xo, axis=-1, keepdims=True) + EPS) * nf_ref[...]


def _final_group(cfg, tile_off, batch_off, nbatch, seq, x1, ys, slot, wgt_t, ada3, g, wgs, wus, wds, nf):
    tm, d, k, ds = cfg.tm_final, cfg.d_model, cfg.top_k, cfg.d_shared
    n = nbatch * seq // tm
    per_batch = seq // tm
    ada = lambda which: pl.BlockSpec((None, 1, d), lambda i: ((batch_off + i // per_batch) * 6 + which, 0, 0))
    const = lambda shape: pl.BlockSpec(shape, lambda i: (0, 0))
    vmem = (2 * k * tm * d * 2 + 2 * 3 * d * ds * 2 + 10 * tm * d * 4) / MIB + 8
    return pl.pallas_call(
        functools.partial(_final_kernel, cfg=cfg), grid=(n,),
        in_specs=[pl.BlockSpec((k, tm), lambda i: (0, tile_off + i), memory_space=pltpu.SMEM),
                  pl.BlockSpec((k, tm), lambda i: (0, tile_off + jnp.minimum(i + 1, n - 1)), memory_space=pltpu.SMEM),
                  pl.BlockSpec((tm, d), lambda i: (tile_off + i, 0)),
                  pl.BlockSpec((tm, k), lambda i: (tile_off + i, 0)),
                  const((1, d)), ada(3), ada(4), ada(5), const((d, ds)), const((d, ds)), const((ds, d)),
                  const((1, d)), pl.BlockSpec(memory_space=pl.ANY)],
        out_specs=pl.BlockSpec((tm, d), lambda i: (i, 0)),
        out_shape=jax.ShapeDtypeStruct((nbatch * seq, d), F32),
        scratch_shapes=[pltpu.VMEM((2 * k * tm, 1, d // 2), U32), pltpu.VMEM((tm, d // 2), U32),
                        pltpu.SemaphoreType.DMA((2,))],
        compiler_params=_params(vmem), name="final")(
            slot, slot, x1, wgt_t, g, ada3, ada3, ada3, wgs, wus, wds, nf, ys)


def _forward(cfg, x_prompt, x_sample, c_prompt, c_sample, w_ada, b_ada, norm_mix, w_in, rel_bias, sink,
             w_attn_proj, w_four, b_four, w_out, norm_ffn, w_router, router_bias, w_gate_e, w_up_e, w_down_e,
             w_gate_s, w_up_s, w_down_s, norm_final):
    d, t = cfg.d_model, cfg.tokens
    aw, kvw, fw = cfg.attn_width, cfg.kv_width, cfg.fourier_width
    x_p = x_prompt.reshape(cfg.tokens_p, d)
    x_s = x_sample.reshape(cfg.tokens_s, d)

    nc = cfg.batch + cfg.dec_batch
    rows = -(-nc // 8) * 8
    c_all = jnp.concatenate([c_prompt, c_sample, jnp.zeros((rows - nc, d), F32)], axis=0)
    ada3 = _ada(cfg, c_all, w_ada[0], b_ada).reshape(rows * 6, 1, d)

    h = _modnorm(cfg, x_p, x_s, norm_mix, ada3)
    w_in_b = w_in[0].astype(BF16)
    c0, c1 = aw + 2 * kvw, aw + 2 * kvw + fw
    qkv = _matmul(h, w_in_b[:, :c0], cfg.tm_mm, c0, "mm_qkv")
    u = _matmul(h, w_in_b[:, c0:c1], cfg.tm_mm, fw, "mm_u")
    gates = _matmul(h, w_in_b[:, c1:], cfg.tm_mm, d, "mm_gates")

    attn = _attention(cfg, qkv, _bias_table(cfg, rel_bias), sink[0])
    u2 = u.reshape(t // cfg.four_n2, cfg.four_n2 * fw)
    f_p = _fourier_group(cfg, u2, 0, cfg.batch, cfg.seq)
    f_s = _fourier_group(cfg, u2, cfg.tokens_p // cfg.four_n2, cfg.dec_batch, cfg.dec_seq)
    mixed = _proj(cfg, attn, f_p, f_s, gates, w_attn_proj[0].astype(BF16), w_four[0].astype(BF16), b_four)
    x1 = _resid(cfg, mixed, w_out[0].astype(BF16), x_p, x_s, ada3)

    hp, idx, wgt, pos, cnt = _router(cfg, x1, norm_ffn, ada3, w_router[0], router_bias[0])
    tme, ne = cfg.tm_expert, cfg.n_experts
    counts = cnt[:, 0]
    padded = (counts + tme - 1) // tme * tme
    pend = jnp.cumsum(padded).astype(I32)
    slot = (pend - padded)[idx] + pos
    n_tiles = t * cfg.top_k // tme + ne
    tile_expert = jnp.minimum(jnp.searchsorted(pend, jnp.arange(n_tiles, dtype=I32) * tme, side="right"),
                              ne - 1).astype(I32)
    n_used = (pend[-1:] // tme).astype(I32)
    xs = _dispatch(cfg, hp, slot, pend, padded, n_used, n_tiles * tme)
    ys = _experts(cfg, xs, tile_expert, n_used, w_gate_e[0], w_up_e[0], w_down_e[0])

    wgt_t = wgt.T
    shared = (w_gate_s[0].astype(BF16), w_up_s[0].astype(BF16), w_down_s[0].astype(BF16))
    nf = norm_final.reshape(1, d)
    y_p = _final_group(cfg, 0, 0, cfg.batch, cfg.seq, x1, ys, slot, wgt_t, ada3, norm_ffn, *shared, nf)
    y_s = _final_group(cfg, cfg.tokens_p // cfg.tm_final, cfg.batch, cfg.dec_batch, cfg.dec_seq, x1, ys, slot,
                       wgt_t, ada3, norm_ffn, *shared, nf)
    return (y_p.reshape(cfg.batch, cfg.seq, d), y_s.reshape(cfg.dec_batch, cfg.dec_seq, d))


def kernel(x_prompt, x_sample, c_prompt, c_sample, w_ada, b_ada, norm_mix, w_in, rel_bias, sink, w_attn_proj,
           w_four, b_four, w_out, norm_ffn, w_router, router_bias, w_gate_e, w_up_e, w_down_e, w_gate_s, w_up_s,
           w_down_s, norm_final):
    return _forward(Cfg(), x_prompt, x_sample, c_prompt, c_sample, w_ada, b_ada, norm_mix, w_in, rel_bias, sink,
                    w_attn_proj, w_four, b_four, w_out, norm_ffn, w_router, router_bias, w_gate_e, w_up_e,
                    w_down_e, w_gate_s, w_up_s, w_down_s, norm_final)
```

```python
import functools
import math
from typing import NamedTuple

import jax
import jax.numpy as jnp
import numpy as np
from jax import lax
from jax.experimental import pallas as pl
from jax.experimental.pallas import tpu as pltpu

F32 = jnp.float32
BF16 = jnp.bfloat16
I32 = jnp.int32
U32 = jnp.uint32

EPS = 1e-6
NEG_INF = -1e30
LANES = 128
MIB = 1024 * 1024


class Cfg(NamedTuple):
    d_model: int = 2048
    batch: int = 4
    seq: int = 8192
    dec_batch: int = 8
    dec_seq: int = 2048
    n_heads: int = 32
    n_kv: int = 4
    head_dim: int = 64
    window: int = 128
    n_buckets: int = 32
    max_distance: int = 128
    n_fgroups: int = 8
    fgroup: int = 128
    n_experts: int = 256
    top_k: int = 8
    n_egroups: int = 8
    topk_groups: int = 4
    d_expert: int = 512
    d_shared: int = 512
    routed_scale: float = 2.5
    tm_norm: int = 512
    tm_mm: int = 512
    tm_proj: int = 256
    tm_router: int = 256
    tm_dispatch: int = 256
    tm_expert: int = 256
    tm_final: int = 256
    four_n2: int = 128
    four_t2: int = 8

    @property
    def attn_width(self):
        return self.n_heads * self.head_dim

    @property
    def kv_width(self):
        return self.n_kv * self.head_dim

    @property
    def fourier_width(self):
        return self.n_fgroups * self.fgroup

    @property
    def tokens_p(self):
        return self.batch * self.seq

    @property
    def tokens_s(self):
        return self.dec_batch * self.dec_seq

    @property
    def tokens(self):
        return self.tokens_p + self.tokens_s


def _params(vmem_mib):
    return pltpu.CompilerParams(vmem_limit_bytes=int(vmem_mib * MIB))


def _batch_of_tile(i, cfg, tm):
    n_p = cfg.tokens_p // tm
    return jnp.where(i < n_p, i // (cfg.seq // tm), cfg.batch + (i - n_p) // (cfg.dec_seq // tm))


def _ada_spec(cfg, tm, which):
    return pl.BlockSpec((None, 1, cfg.d_model), lambda i: (_batch_of_tile(i, cfg, tm) * 6 + which, 0, 0))


def _two_group_specs(cfg, tm, width):
    n_p = cfg.tokens_p // tm
    return [pl.BlockSpec((tm, width), lambda i: (jnp.minimum(i, n_p - 1), 0)),
            pl.BlockSpec((tm, width), lambda i: (jnp.maximum(i - n_p, 0), 0))]


def _modulated_norm(x, g, shift, scale):
    y = x * lax.rsqrt(jnp.mean(x * x, axis=-1, keepdims=True) + EPS) * g
    return y * (1.0 + scale) + shift


def _pack_bf16_pair(lo, hi):
    lo_bits = pltpu.bitcast(lo.astype(BF16).astype(F32), U32) >> 16
    hi_bits = pltpu.bitcast(hi.astype(BF16).astype(F32), U32) & jnp.uint32(0xFFFF0000)
    return hi_bits | lo_bits


def _unpack_bf16_pair(p):
    lo = pltpu.bitcast(p << 16, F32)
    hi = pltpu.bitcast(p & jnp.uint32(0xFFFF0000), F32)
    return lo, hi


def _ada_kernel(c_ref, w_ref, b_ref, o_ref):
    c = c_ref[...]
    s = c * jax.nn.sigmoid(c)
    o_ref[...] = jnp.dot(s, w_ref[...], preferred_element_type=F32, precision=lax.Precision.HIGHEST) + b_ref[...]


def _ada(cfg, c_all, w_ada, b_ada):
    rows, d = c_all.shape
    n = w_ada.shape[1]
    tn = n // 8
    return pl.pallas_call(
        _ada_kernel, grid=(n // tn,),
        in_specs=[pl.BlockSpec((rows, d), lambda j: (0, 0)), pl.BlockSpec((d, tn), lambda j: (0, j)),
                  pl.BlockSpec((1, tn), lambda j: (0, j))],
        out_specs=pl.BlockSpec((rows, tn), lambda j: (0, j)),
        out_shape=jax.ShapeDtypeStruct((rows, n), F32),
        compiler_params=_params(2 * d * tn * 4 / MIB + 8), name="ada")(c_all, w_ada, b_ada)


def _modnorm_kernel(xp_ref, xs_ref, g_ref, sh_ref, sc_ref, o_ref, *, n_p):
    i = pl.program_id(0)

    def run(x_ref):
        o_ref[...] = _modulated_norm(x_ref[...], g_ref[...], sh_ref[...], sc_ref[...]).astype(o_ref.dtype)

    pl.when(i < n_p)(lambda: run(xp_ref))
    pl.when(i >= n_p)(lambda: run(xs_ref))


def _modnorm(cfg, x_p, x_s, g, ada3):
    tm, d = cfg.tm_norm, cfg.d_model
    return pl.pallas_call(
        functools.partial(_modnorm_kernel, n_p=cfg.tokens_p // tm), grid=(cfg.tokens // tm,),
        in_specs=_two_group_specs(cfg, tm, d) + [pl.BlockSpec((1, d), lambda i: (0, 0)),
                                                 _ada_spec(cfg, tm, 0), _ada_spec(cfg, tm, 1)],
        out_specs=pl.BlockSpec((tm, d), lambda i: (i, 0)),
        out_shape=jax.ShapeDtypeStruct((cfg.tokens, d), BF16),
        compiler_params=_params(7 * tm * d * 4 / MIB + 8), name="modnorm")(x_p, x_s, g, ada3, ada3)


def _mm_kernel(a_ref, w_ref, o_ref):
    o_ref[...] = jnp.dot(a_ref[...], w_ref[...], preferred_element_type=F32).astype(o_ref.dtype)


def _matmul(a, w, tm, tn, name):
    m, k = a.shape
    n = w.shape[1]
    vmem = (2 * tm * k * 2 + 2 * k * tn * 2 + 2 * tm * tn * 2 + tm * tn * 4) / MIB + 8
    return pl.pallas_call(
        _mm_kernel, grid=(n // tn, m // tm),
        in_specs=[pl.BlockSpec((tm, k), lambda j, i: (i, 0)), pl.BlockSpec((k, tn), lambda j, i: (0, j))],
        out_specs=pl.BlockSpec((tm, tn), lambda j, i: (i, j)),
        out_shape=jax.ShapeDtypeStruct((m, n), BF16),
        compiler_params=_params(vmem), name=name)(a, w)


def _t5_bucket_table(cfg):
    blk, span = cfg.window, 3 * cfg.window
    rel = (np.arange(span)[None, :] - cfg.window - np.arange(blk)[:, None]).astype(np.int32)
    nb = cfg.n_buckets // 2
    max_exact = nb // 2
    ret = np.where(rel > 0, nb, 0)
    n = np.abs(rel)
    nf = np.maximum(n, 1).astype(np.float32)
    ratio = np.log(nf / np.float32(max_exact)) / np.float32(math.log(cfg.max_distance / max_exact))
    large = max_exact + (ratio * np.float32(nb - max_exact)).astype(np.int32)
    large = np.minimum(large, nb - 1)
    return (ret + np.where(n < max_exact, n, large)).astype(np.int32), rel


def _bias_kernel(bucket_ref, rbt_ref, o_ref, *, n_buckets):
    bucket = bucket_ref[...]
    acc = jnp.zeros(o_ref.shape, F32)
    for b in range(n_buckets):
        acc = jnp.where(bucket == b, rbt_ref[:, b:b + 1], acc)
    o_ref[...] = acc


def _bias_table(cfg, rel_bias):
    buckets, _ = _t5_bucket_table(cfg)
    blk, span, h = cfg.window, 3 * cfg.window, cfg.n_heads
    n = blk * span
    flat = pl.pallas_call(
        functools.partial(_bias_kernel, n_buckets=cfg.n_buckets),
        out_shape=jax.ShapeDtypeStruct((h, n), F32), name="bias")(
            jnp.asarray(buckets.reshape(1, n)), rel_bias.T.astype(F32))
    pairs = h // cfg.n_kv // 2
    t = flat.reshape(cfg.n_kv, pairs, 2, blk, span)
    return jnp.transpose(t, (0, 1, 3, 2, 4)).reshape(cfg.n_kv, pairs * blk, 2 * span)


def _attn_kernel(q_ref, kp_ref, kc_ref, kn_ref, vp_ref, vc_ref, vn_ref, bias_ref, sink_ref, rel_ref, kofs_ref,
                 o_ref, valid_ref, *, cfg, blocks_p):
    blk, hd = cfg.window, cfg.head_dim
    span = 3 * blk
    pairs = cfg.n_heads // cfg.n_kv // 2
    nb_p, nb_s = cfg.seq // blk, cfg.dec_seq // blk
    i = pl.program_id(0)
    in_p = i < blocks_p
    pos = jnp.where(in_p, i % nb_p, (i - blocks_p) % nb_s)
    seq_len = jnp.where(in_p, cfg.seq, cfg.dec_seq)
    kpos = pos * blk + kofs_ref[...]
    in_band = jnp.abs(rel_ref[...]) <= cfg.window
    in_seq = (kpos >= 0) & (kpos < seq_len)
    valid_ref[...] = jnp.where(in_band, jnp.where(in_seq, 1.0, 0.0), 0.0)

    k = jnp.concatenate([kp_ref[...], kc_ref[...], kn_ref[...]], axis=0)
    v = jnp.concatenate([vp_ref[...], vc_ref[...], vn_ref[...]], axis=0)
    scale = hd ** -0.5
    lane = lax.broadcasted_iota(I32, (pairs * blk, 2 * hd), 1)
    for j in range(cfg.n_kv):
        kj = k[:, j * hd:(j + 1) * hd]
        vj = v[:, j * hd:(j + 1) * hd]
        zero = jnp.zeros_like(kj)
        k2 = jnp.concatenate([jnp.concatenate([kj, zero], axis=1), jnp.concatenate([zero, kj], axis=1)], axis=0)
        v2 = jnp.concatenate([jnp.concatenate([vj, zero], axis=1), jnp.concatenate([zero, vj], axis=1)], axis=0)
        base = j * pairs * 2 * hd
        qs = jnp.concatenate([q_ref[:, base + p * 2 * hd: base + (p + 1) * 2 * hd] for p in range(pairs)], axis=0)
        logits = lax.dot_general(qs, k2, (((1,), (1,)), ((), ())), preferred_element_type=F32)
        logits = logits * scale + bias_ref[j]
        logits = jnp.where(valid_ref[...] > 0.5, logits, NEG_INF)
        sink = sink_ref[j]
        probs, inv = [], []
        for half in range(2):
            lh = logits[:, half * span:(half + 1) * span]
            sk = sink[:, half:half + 1]
            m = jnp.maximum(jnp.max(lh, axis=-1, keepdims=True), sk)
            p = jnp.exp(lh - m)
            denom = jnp.sum(p, axis=-1, keepdims=True) + jnp.exp(sk - m)
            probs.append(p.astype(BF16))
            inv.append(1.0 / denom)
        pv = jnp.dot(jnp.concatenate(probs, axis=1), v2, preferred_element_type=F32)
        out = pv * jnp.where(lane < hd, inv[0], inv[1])
        for p in range(pairs):
            o_ref[:, base + p * 2 * hd: base + (p + 1) * 2 * hd] = out[p * blk:(p + 1) * blk].astype(o_ref.dtype)


def _attention(cfg, qkv, bias_tbl, sink):
    blk, hd = cfg.window, cfg.head_dim
    aw, kvw = cfg.attn_width, cfg.kv_width
    pairs = cfg.n_heads // cfg.n_kv // 2
    nb_p, nb_s = cfg.seq // blk, cfg.dec_seq // blk
    blocks_p = cfg.batch * nb_p
    n_blocks = cfg.tokens // blk
    _, rel = _t5_bucket_table(cfg)
    rel_tbl = np.tile(rel, (pairs, 2)).astype(np.int32)
    kofs_tbl = np.tile((np.arange(3 * blk) - cfg.window)[None, :], (pairs * blk, 2)).astype(np.int32)
    sink_tbl = jnp.broadcast_to(sink.astype(F32).reshape(cfg.n_kv, pairs, 1, 2),
                                (cfg.n_kv, pairs, blk, 2)).reshape(cfg.n_kv, pairs * blk, 2)

    def seq_pos(i):
        in_p = i < blocks_p
        return jnp.where(in_p, i % nb_p, (i - blocks_p) % nb_s), jnp.where(in_p, nb_p, nb_s)

    def prev_blk(i):
        pos, _ = seq_pos(i)
        return jnp.where(pos == 0, i, i - 1)

    def next_blk(i):
        pos, nb = seq_pos(i)
        return jnp.where(pos == nb - 1, i, i + 1)

    kcol, vcol = aw // kvw, aw // kvw + 1
    full = lambda shape: pl.BlockSpec(shape, lambda i: (0,) * len(shape))
    return pl.pallas_call(
        functools.partial(_attn_kernel, cfg=cfg, blocks_p=blocks_p), grid=(n_blocks,),
        in_specs=[pl.BlockSpec((blk, aw), lambda i: (i, 0)),
                  pl.BlockSpec((blk, kvw), lambda i: (prev_blk(i), kcol)),
                  pl.BlockSpec((blk, kvw), lambda i: (i, kcol)),
                  pl.BlockSpec((blk, kvw), lambda i: (next_blk(i), kcol)),
                  pl.BlockSpec((blk, kvw), lambda i: (prev_blk(i), vcol)),
                  pl.BlockSpec((blk, kvw), lambda i: (i, vcol)),
                  pl.BlockSpec((blk, kvw), lambda i: (next_blk(i), vcol)),
                  full(bias_tbl.shape), full(sink_tbl.shape), full(rel_tbl.shape), full(kofs_tbl.shape)],
        out_specs=pl.BlockSpec((blk, aw), lambda i: (i, 0)),
        out_shape=jax.ShapeDtypeStruct((cfg.tokens, aw), BF16),
        scratch_shapes=[pltpu.VMEM(rel_tbl.shape, F32)],
        compiler_params=_params(40), name="attn")(
            qkv, qkv, qkv, qkv, qkv, qkv, qkv, bias_tbl, sink_tbl, jnp.asarray(rel_tbl), jnp.asarray(kofs_tbl))


def _dft_cos_sin(n):
    ang = 2.0 * np.pi * np.outer(np.arange(n), np.arange(n)) / n
    return np.cos(ang), np.sin(ang)


def _four1_kernel(u_ref, m1_ref, twr_ref, twi_ref, zr_ref, zi_ref, *, n1, t2, width):
    y = jnp.dot(m1_ref[...], u_ref[...], preferred_element_type=F32)
    twr, twi = twr_ref[0], twi_ref[0]
    for l in range(t2):
        sl = slice(l * width, (l + 1) * width)
        yr, yi = y[:n1, sl], y[n1:, sl]
        cr, ci = twr[:, l:l + 1], twi[:, l:l + 1]
        zr_ref[:, sl] = (yr * cr - yi * ci).astype(zr_ref.dtype)
        zi_ref[:, sl] = (yr * ci + yi * cr).astype(zi_ref.dtype)


def _four3_kernel(zr_ref, zi_ref, m3_ref, mc_ref, o_ref, *, n2, n_groups, group, norm):
    z = jnp.concatenate([zr_ref[0], zi_ref[0]], axis=0)
    x = jnp.dot(m3_ref[...], z, preferred_element_type=F32)
    xr, xi = x[:n2].astype(BF16), x[n2:].astype(BF16)
    for g in range(n_groups):
        sl = slice(g * group, (g + 1) * group)
        xg = jnp.concatenate([xr[:, sl], xi[:, sl]], axis=1)
        o_ref[:, sl] = (jnp.dot(xg, mc_ref[...], preferred_element_type=F32) * norm).astype(o_ref.dtype)


def _fourier_group(cfg, u2, row_off, nbatch, seq):
    n2, t2, width = cfg.four_n2, cfg.four_t2, cfg.fourier_width
    n1 = seq // n2
    c1, s1 = _dft_cos_sin(n1)
    m1 = jnp.asarray(np.concatenate([c1, -s1], axis=0), BF16)
    ang = 2.0 * np.pi * np.outer(np.arange(n1), np.arange(n2)) / seq
    twr = jnp.asarray(np.cos(ang).reshape(n1, n2 // t2, t2).transpose(1, 0, 2), F32)
    twi = jnp.asarray((-np.sin(ang)).reshape(n1, n2 // t2, t2).transpose(1, 0, 2), F32)
    blk_off = row_off // n1
    zshape = jax.ShapeDtypeStruct((nbatch * n1, n2 * width), BF16)
    tw_spec = pl.BlockSpec((1, n1, t2), lambda b, j: (j, 0, 0))
    zr, zi = pl.pallas_call(
        functools.partial(_four1_kernel, n1=n1, t2=t2, width=width), grid=(nbatch, n2 // t2),
        in_specs=[pl.BlockSpec((n1, t2 * width), lambda b, j: (blk_off + b, j)),
                  pl.BlockSpec((2 * n1, n1), lambda b, j: (0, 0)), tw_spec, tw_spec],
        out_specs=[pl.BlockSpec((n1, t2 * width), lambda b, j: (b, j))] * 2,
        out_shape=[zshape, zshape],
        compiler_params=_params(32), name="four1")(u2, m1, twr, twi)

    c2, s2 = _dft_cos_sin(n2)
    m3 = jnp.asarray(np.block([[c2, s2], [-s2, c2]]), BF16)
    cc, sc = _dft_cos_sin(cfg.fgroup)
    mc = jnp.asarray(np.concatenate([cc, sc], axis=0), BF16)
    norm = 1.0 / math.sqrt(seq * cfg.fgroup)
    z_spec = pl.BlockSpec((1, n2, width), lambda b, k1: (b * n1 + k1, 0, 0))
    f = pl.pallas_call(
        functools.partial(_four3_kernel, n2=n2, n_groups=cfg.n_fgroups, group=cfg.fgroup, norm=norm),
        grid=(nbatch, n1),
        in_specs=[z_spec, z_spec, pl.BlockSpec((2 * n2, 2 * n2), lambda b, k1: (0, 0)),
                  pl.BlockSpec((2 * cfg.fgroup, cfg.fgroup), lambda b, k1: (0, 0))],
        out_specs=pl.BlockSpec((n2, width), lambda b, k1: (b, k1)),
        out_shape=jax.ShapeDtypeStruct((nbatch * n2, n1 * width), BF16),
        compiler_params=_params(32), name="four3")(
            zr.reshape(nbatch * n1, n2, width), zi.reshape(nbatch * n1, n2, width), m3, mc)
    return f.reshape(nbatch * seq, width)


def _proj_kernel(a_ref, fp_ref, fs_ref, g_ref, wa_ref, wf_ref, bf_ref, o_ref, *, n_p, d):
    i = pl.program_id(0)
    a = jnp.dot(a_ref[...], wa_ref[...], preferred_element_type=F32)

    def run(f_ref):
        f = jnp.dot(f_ref[...], wf_ref[...], preferred_element_type=F32) + bf_ref[...]
        ga = g_ref[:, :d].astype(F32)
        gf = g_ref[:, d:].astype(F32)
        o_ref[...] = (jax.nn.sigmoid(ga) * a + jax.nn.sigmoid(gf) * f).astype(o_ref.dtype)

    pl.when(i < n_p)(lambda: run(fp_ref))
    pl.when(i >= n_p)(lambda: run(fs_ref))


def _proj(cfg, attn, f_p, f_s, gates, wa, wf, bf):
    tm, d, aw, fw = cfg.tm_proj, cfg.d_model, cfg.attn_width, cfg.fourier_width
    const = lambda shape: pl.BlockSpec(shape, lambda i: (0, 0))
    vmem = (2 * (aw + fw) * d * 2 + 2 * tm * (aw + 2 * fw + 3 * d) * 2 + 3 * tm * d * 4) / MIB + 8
    return pl.pallas_call(
        functools.partial(_proj_kernel, n_p=cfg.tokens_p // tm, d=d), grid=(cfg.tokens // tm,),
        in_specs=[pl.BlockSpec((tm, aw), lambda i: (i, 0))] + _two_group_specs(cfg, tm, fw) + [
            pl.BlockSpec((tm, 2 * d), lambda i: (i, 0)), const((aw, d)), const((fw, d)), const((1, d))],
        out_specs=pl.BlockSpec((tm, d), lambda i: (i, 0)),
        out_shape=jax.ShapeDtypeStruct((cfg.tokens, d), BF16),
        compiler_params=_params(vmem), name="proj")(attn, f_p, f_s, gates, wa, wf, bf)


def _resid_kernel(m_ref, wo_ref, xp_ref, xs_ref, gate_ref, o_ref, *, n_p):
    i = pl.program_id(0)
    out = jnp.dot(m_ref[...], wo_ref[...], preferred_element_type=F32)

    def run(x_ref):
        o_ref[...] = x_ref[...] + gate_ref[...] * out

    pl.when(i < n_p)(lambda: run(xp_ref))
    pl.when(i >= n_p)(lambda: run(xs_ref))


def _resid(cfg, mixed, wo, x_p, x_s, ada3):
    tm, d = cfg.tm_proj, cfg.d_model
    vmem = (2 * d * d * 2 + 2 * tm * d * 2 + 7 * tm * d * 4) / MIB + 8
    return pl.pallas_call(
        functools.partial(_resid_kernel, n_p=cfg.tokens_p // tm), grid=(cfg.tokens // tm,),
        in_specs=[pl.BlockSpec((tm, d), lambda i: (i, 0)), pl.BlockSpec((d, d), lambda i: (0, 0))]
        + _two_group_specs(cfg, tm, d) + [_ada_spec(cfg, tm, 2)],
        out_specs=pl.BlockSpec((tm, d), lambda i: (i, 0)),
        out_shape=jax.ShapeDtypeStruct((cfg.tokens, d), F32),
        compiler_params=_params(vmem), name="resid")(mixed, wo, x_p, x_s, ada3)


def _router_kernel(x_ref, g_ref, sh_ref, sc_ref, wrt_ref, rb_ref, tri_ref,
                   hp_ref, idx_ref, wgt_ref, pos_ref, cnt_ref, carry_ref, *, cfg):
    d, ne, ng = cfg.d_model, cfg.n_experts, cfg.n_egroups
    per = ne // ng
    i = pl.program_id(0)

    @pl.when(i == 0)
    def _():
        carry_ref[...] = jnp.zeros_like(carry_ref)

    h = _modulated_norm(x_ref[...], g_ref[...], sh_ref[...], sc_ref[...])
    tm = h.shape[0]
    hp_ref[...] = _pack_bf16_pair(h[:, :d // 2], h[:, d // 2:]).reshape(tm, 1, d // 2)

    logits = lax.dot_general(wrt_ref[...], h, (((1,), (1,)), ((), ())), preferred_element_type=F32,
                             precision=lax.Precision.HIGHEST)
    scores = jax.nn.sigmoid(logits)
    sel = scores + rb_ref[...]
    s3 = sel.reshape(ng, per, tm)
    io3 = lax.broadcasted_iota(I32, (ng, per, tm), 1)
    m1 = jnp.max(s3, axis=1, keepdims=True)
    i1 = jnp.min(jnp.where(s3 == m1, io3, per), axis=1, keepdims=True)
    m2 = jnp.max(jnp.where(io3 == i1, -jnp.inf, s3), axis=1, keepdims=True)
    gs = (m1 + m2).reshape(ng, tm)
    gio = lax.broadcasted_iota(I32, (ng, tm), 0)
    before = jnp.zeros((ng, tm), I32)
    for g2 in range(ng):
        row = gs[g2:g2 + 1, :]
        ahead = jnp.where(row > gs, 1, jnp.where(row == gs, jnp.where(gio > g2, 1, 0), 0))
        before = before + ahead
    keep = jnp.where(before < cfg.topk_groups, 1.0, 0.0)
    keep_e = jnp.broadcast_to(keep.reshape(ng, 1, tm), (ng, per, tm)).reshape(ne, tm)
    cur = jnp.where(keep_e > 0.5, sel, -jnp.inf)

    eio = lax.broadcasted_iota(I32, (ne, tm), 0)
    chosen, picked = [], jnp.zeros((ne, tm), F32)
    for _ in range(cfg.top_k):
        m = jnp.max(cur, axis=0, keepdims=True)
        ij = jnp.min(jnp.where(cur == m, eio, ne), axis=0, keepdims=True)
        hit = eio == ij
        picked = jnp.where(hit, 1.0, picked)
        cur = jnp.where(hit, -jnp.inf, cur)
        chosen.append(ij)
    before_t = jnp.dot(picked.astype(BF16), tri_ref[...], preferred_element_type=F32)
    rank = carry_ref[...] + before_t
    sc_rows, pos_rows = [], []
    for ij in chosen:
        hit = eio == ij
        sc_rows.append(jnp.sum(jnp.where(hit, scores, 0.0), axis=0, keepdims=True))
        pos_rows.append(jnp.sum(jnp.where(hit, rank, 0.0), axis=0, keepdims=True))
    sc_all = jnp.concatenate(sc_rows, axis=0)
    total = jnp.sum(sc_all, axis=0, keepdims=True)
    idx_ref[...] = jnp.concatenate(chosen, axis=0)
    wgt_ref[...] = sc_all / total * cfg.routed_scale
    pos_ref[...] = jnp.concatenate(pos_rows, axis=0).astype(I32)
    carry_ref[...] = carry_ref[...] + jnp.sum(picked, axis=1, keepdims=True)
    cnt_ref[...] = carry_ref[...].astype(I32)


def _router(cfg, x1, g, ada3, w_router, router_bias):
    tm, d, ne, k = cfg.tm_router, cfg.d_model, cfg.n_experts, cfg.top_k
    t = cfg.tokens
    tri = jnp.asarray(np.triu(np.ones((tm, tm), np.float32), 1), BF16)
    row = lambda dt: jax.ShapeDtypeStruct((k, t), dt)
    row_spec = pl.BlockSpec((k, tm), lambda i: (0, i))
    return pl.pallas_call(
        functools.partial(_router_kernel, cfg=cfg), grid=(t // tm,),
        in_specs=[pl.BlockSpec((tm, d), lambda i: (i, 0)), pl.BlockSpec((1, d), lambda i: (0, 0)),
                  _ada_spec(cfg, tm, 3), _ada_spec(cfg, tm, 4),
                  pl.BlockSpec((ne, d), lambda i: (0, 0)), pl.BlockSpec((ne, 1), lambda i: (0, 0)),
                  pl.BlockSpec((tm, tm), lambda i: (0, 0))],
        out_specs=[pl.BlockSpec((tm, 1, d // 2), lambda i: (i, 0, 0)), row_spec, row_spec, row_spec,
                   pl.BlockSpec((ne, 1), lambda i: (0, 0))],
        out_shape=[jax.ShapeDtypeStruct((t, 1, d // 2), U32), row(I32), row(F32), row(I32),
                   jax.ShapeDtypeStruct((ne, 1), I32)],
        scratch_shapes=[pltpu.VMEM((ne, 1), F32)],
        compiler_params=_params(40), name="router")(
            x1, g, ada3, ada3, w_router.T.astype(F32), router_bias.reshape(ne, 1).astype(F32), tri)


def _slots_kernel(pstart_ref, idx_ref, pos_ref, o_ref, *, n_experts):
    idx = idx_ref[...]
    base = lax.fori_loop(0, n_experts, lambda e, acc: jnp.where(idx == e, pstart_ref[e], acc),
                         jnp.zeros(idx.shape, I32))
    o_ref[...] = base + pos_ref[...]


def _slots(cfg, pstart, idx, pos):
    k, t = idx.shape
    tn = math.gcd(t, 4096)
    spec = pl.BlockSpec((k, tn), lambda i, *_: (0, i))
    grid_spec = pltpu.PrefetchScalarGridSpec(num_scalar_prefetch=1, grid=(t // tn,), in_specs=[spec, spec],
                                             out_specs=spec)
    return pl.pallas_call(functools.partial(_slots_kernel, n_experts=cfg.n_experts), grid_spec=grid_spec,
                          out_shape=jax.ShapeDtypeStruct((k, t), I32), name="slots")(pstart, idx, pos)


def _dispatch_kernel(pend_ref, padded_ref, nt_ref, hp_ref, slot_ref, xs_ref, zero_ref, sem, zsem, *, cfg):
    tm, tme, k, ne = cfg.tm_dispatch, cfg.tm_expert, cfg.top_k, cfg.n_experts
    n_tiles = xs_ref.shape[0] // tme
    i = pl.program_id(0)

    def pad_copy(e):
        return pltpu.make_async_copy(zero_ref, xs_ref.at[pl.ds(pend_ref[e] - tme, tme)], zsem)

    def tail_copy(t):
        return pltpu.make_async_copy(zero_ref, xs_ref.at[pl.ds(t * tme, tme)], zsem)

    @pl.when(i == 0)
    def _():
        zero_ref[...] = jnp.zeros_like(zero_ref)

        def start(e, c):
            pl.when(padded_ref[e] > 0)(lambda: pad_copy(e).start())
            return c

        def wait(e, c):
            pl.when(padded_ref[e] > 0)(lambda: pad_copy(e).wait())
            return c

        lax.fori_loop(0, ne, start, 0)
        lax.fori_loop(nt_ref[0], n_tiles, lambda t, c: (tail_copy(t).start(), c)[1], 0)
        lax.fori_loop(0, ne, wait, 0)
        lax.fori_loop(nt_ref[0], n_tiles, lambda t, c: (tail_copy(t).wait(), c)[1], 0)

    def row(r, c):
        for j in range(k):
            pltpu.make_async_copy(hp_ref.at[r], xs_ref.at[slot_ref[j, r]], sem).start(priority=j % 2)
        return c

    lax.fori_loop(0, tm, row, 0)
    for _ in range(k):
        pltpu.make_async_copy(hp_ref, xs_ref.at[pl.ds(0, tm)], sem).wait()


def _dispatch(cfg, hp, slot, pend, padded, n_used, n_slots):
    tm, d, k = cfg.tm_dispatch, cfg.d_model, cfg.top_k
    grid_spec = pltpu.PrefetchScalarGridSpec(
        num_scalar_prefetch=3, grid=(cfg.tokens // tm,),
        in_specs=[pl.BlockSpec((tm, 1, d // 2), lambda i, *_: (i, 0, 0)),
                  pl.BlockSpec((k, tm), lambda i, *_: (0, i), memory_space=pltpu.SMEM)],
        out_specs=pl.BlockSpec(memory_space=pl.ANY),
        scratch_shapes=[pltpu.VMEM((cfg.tm_expert, 1, d // 2), U32), pltpu.SemaphoreType.DMA,
                        pltpu.SemaphoreType.DMA])
    return pl.pallas_call(
        functools.partial(_dispatch_kernel, cfg=cfg), grid_spec=grid_spec,
        out_shape=jax.ShapeDtypeStruct((n_slots, 1, d // 2), U32),
        compiler_params=_params(24), name="dispatch")(pend, padded, n_used, hp, slot)


def _expert_kernel(te_ref, nt_ref, nxt_ref, xs_ref, wg_hbm, wu_hbm, wd_hbm, ys_ref,
                   wgf, wuf, wdf, wgb, wub, wdb, rows2d, side_ref, wsem):
    i = pl.program_id(0)
    tme, _, half = xs_ref.shape

    def weight_copies(e, s):
        return [pltpu.make_async_copy(src.at[e], dst.at[s], wsem.at[s])
                for src, dst in ((wg_hbm, wgf), (wu_hbm, wuf), (wd_hbm, wdf))]

    @pl.when(i >= nt_ref[0])
    def _():
        ys_ref[...] = jnp.zeros_like(ys_ref)

    @pl.when(i < nt_ref[0])
    def _():
        e = te_ref[i]

        @pl.when(i == 0)
        def _():
            side_ref[0] = 0
            for c in weight_copies(e, 0):
                c.start()

        @pl.when(jnp.logical_or(i == 0, e != te_ref[jnp.maximum(i - 1, 0)]))
        def _():
            s = side_ref[0]
            nxt = nxt_ref[e]
            @pl.when(nxt >= 0)
            def _():
                for c in weight_copies(nxt, 1 - s):
                    c.start()

            for c in weight_copies(e, s):
                c.wait()
            wgb[...] = wgf[s].astype(BF16)
            wub[...] = wuf[s].astype(BF16)
            wdb[...] = wdf[s].astype(BF16)
            side_ref[0] = 1 - s

        rows2d[...] = xs_ref[...].reshape(tme, half)
        lo, hi = _unpack_bf16_pair(rows2d[...])
        x = jnp.concatenate([lo.astype(BF16), hi.astype(BF16)], axis=1)
        g = jnp.dot(x, wgb[...], preferred_element_type=F32)
        u = jnp.dot(x, wub[...], preferred_element_type=F32)
        a = (g * jax.nn.sigmoid(g) * u).astype(BF16)
        y = jnp.dot(a, wdb[...], preferred_element_type=F32)
        ys_ref[...] = _pack_bf16_pair(y[:, :half], y[:, half:]).reshape(tme, 1, half)


def _experts(cfg, xs, tile_expert, n_used, next_expert, wg, wu, wd):
    tme, d, de = cfg.tm_expert, cfg.d_model, cfg.d_expert
    n_tiles = xs.shape[0] // tme
    any_spec = pl.BlockSpec(memory_space=pl.ANY)
    grid_spec = pltpu.PrefetchScalarGridSpec(
        num_scalar_prefetch=3, grid=(n_tiles,),
        in_specs=[pl.BlockSpec((tme, 1, d // 2), lambda i, te, nt, nx: (jnp.minimum(i, nt[0] - 1), 0, 0)),
                  any_spec, any_spec, any_spec],
        out_specs=pl.BlockSpec((tme, 1, d // 2), lambda i, te, nt, nx: (i, 0, 0)),
        scratch_shapes=[pltpu.VMEM((2, d, de), F32), pltpu.VMEM((2, d, de), F32), pltpu.VMEM((2, de, d), F32),
                        pltpu.VMEM((d, de), BF16), pltpu.VMEM((d, de), BF16), pltpu.VMEM((de, d), BF16),
                        pltpu.VMEM((tme, d // 2), U32), pltpu.SMEM((1,), I32), pltpu.SemaphoreType.DMA((2,))])
    vmem = (2 * 3 * d * de * 4 + 3 * d * de * 2 + 4 * tme * d * 2 + 6 * tme * d * 4) / MIB + 8
    return pl.pallas_call(
        _expert_kernel, grid_spec=grid_spec, out_shape=jax.ShapeDtypeStruct(xs.shape, U32),
        compiler_params=_params(vmem), name="experts")(tile_expert, n_used, next_expert, xs, wg, wu, wd)


def _final_kernel(slot_ref, slotn_ref, x_ref, wt_ref, g_ref, sh_ref, sc_ref, gate_ref, wg_ref, wu_ref, wd_ref,
                  nf_ref, ys_ref, o_ref, buf, rows2d, sem, *, cfg):
    tm, k = cfg.tm_final, cfg.top_k
    half = rows2d.shape[1]
    i = pl.program_id(0)
    n = pl.num_programs(0)

    def plane(b, j):
        return pl.ds(pl.multiple_of((b * k + j) * tm, tm), tm)

    def gather(slots, b):
        def row(r, c):
            for j in range(k):
                pltpu.make_async_copy(ys_ref.at[slots[j, r]], buf.at[(b * k + j) * tm + r],
                                      sem.at[b]).start(priority=j % 2)
            return c

        lax.fori_loop(0, tm, row, 0)

    pl.when(i == 0)(lambda: gather(slot_ref, 0))
    pl.when(i + 1 < n)(lambda: gather(slotn_ref, (i + 1) % 2))
    b = i % 2
    for j in range(k):
        pltpu.make_async_copy(ys_ref.at[pl.ds(0, tm)], buf.at[plane(b, j)], sem.at[b]).wait()

    x = x_ref[...]
    h = _modulated_norm(x, g_ref[...], sh_ref[...], sc_ref[...]).astype(BF16)
    g = jnp.dot(h, wg_ref[...], preferred_element_type=F32)
    u = jnp.dot(h, wu_ref[...], preferred_element_type=F32)
    y = jnp.dot((g * jax.nn.sigmoid(g) * u).astype(BF16), wd_ref[...], preferred_element_type=F32)
    for j in range(k):
        rows2d[...] = buf[plane(b, j)].reshape(tm, half)
        lo, hi = _unpack_bf16_pair(rows2d[...])
        y = y + wt_ref[:, j:j + 1] * jnp.concatenate([lo, hi], axis=1)
    xo = x + gate_ref[...] * y
    o_ref[...] = xo * lax.rsqrt(jnp.mean(xo * xo, axis=-1, keepdims=True) + EPS) * nf_ref[...]


def _final_group(cfg, tile_off, batch_off, nbatch, seq, x1, ys, slot, wgt_t, ada3, g, wgs, wus, wds, nf):
    tm, d, k, ds = cfg.tm_final, cfg.d_model, cfg.top_k, cfg.d_shared
    n = nbatch * seq // tm
    per_batch = seq // tm
    ada = lambda which: pl.BlockSpec((None, 1, d), lambda i: ((batch_off + i // per_batch) * 6 + which, 0, 0))
    const = lambda shape: pl.BlockSpec(shape, lambda i: (0, 0))
    vmem = (2 * k * tm * d * 2 + 2 * 3 * d * ds * 2 + 10 * tm * d * 4) / MIB + 8
    return pl.pallas_call(
        functools.partial(_final_kernel, cfg=cfg), grid=(n,),
        in_specs=[pl.BlockSpec((k, tm), lambda i: (0, tile_off + i), memory_space=pltpu.SMEM),
                  pl.BlockSpec((k, tm), lambda i: (0, tile_off + jnp.minimum(i + 1, n - 1)), memory_space=pltpu.SMEM),
                  pl.BlockSpec((tm, d), lambda i: (tile_off + i, 0)),
                  pl.BlockSpec((tm, k), lambda i: (tile_off + i, 0)),
                  const((1, d)), ada(3), ada(4), ada(5), const((d, ds)), const((d, ds)), const((ds, d)),
                  const((1, d)), pl.BlockSpec(memory_space=pl.ANY)],
        out_specs=pl.BlockSpec((tm, d), lambda i: (i, 0)),
        out_shape=jax.ShapeDtypeStruct((nbatch * seq, d), F32),
        scratch_shapes=[pltpu.VMEM((2 * k * tm, 1, d // 2), U32), pltpu.VMEM((tm, d // 2), U32),
                        pltpu.SemaphoreType.DMA((2,))],
        compiler_params=_params(vmem), name="final")(
            slot, slot, x1, wgt_t, g, ada3, ada3, ada3, wgs, wus, wds, nf, ys)


def _forward(cfg, x_prompt, x_sample, c_prompt, c_sample, w_ada, b_ada, norm_mix, w_in, rel_bias, sink,
             w_attn_proj, w_four, b_four, w_out, norm_ffn, w_router, router_bias, w_gate_e, w_up_e, w_down_e,
             w_gate_s, w_up_s, w_down_s, norm_final):
    d, t = cfg.d_model, cfg.tokens
    aw, kvw, fw = cfg.attn_width, cfg.kv_width, cfg.fourier_width
    x_p = x_prompt.reshape(cfg.tokens_p, d)
    x_s = x_sample.reshape(cfg.tokens_s, d)

    nc = cfg.batch + cfg.dec_batch
    rows = -(-nc // 8) * 8
    c_all = jnp.concatenate([c_prompt, c_sample, jnp.zeros((rows - nc, d), F32)], axis=0)
    ada3 = _ada(cfg, c_all, w_ada[0], b_ada).reshape(rows * 6, 1, d)

    h = _modnorm(cfg, x_p, x_s, norm_mix, ada3)
    w_in_b = w_in[0].astype(BF16)
    c0, c1 = aw + 2 * kvw, aw + 2 * kvw + fw
    qkv = _matmul(h, w_in_b[:, :c0], cfg.tm_mm, c0, "mm_qkv")
    u = _matmul(h, w_in_b[:, c0:c1], cfg.tm_mm, fw, "mm_u")
    gates = _matmul(h, w_in_b[:, c1:], cfg.tm_mm, d, "mm_gates")

    attn = _attention(cfg, qkv, _bias_table(cfg, rel_bias), sink[0])
    u2 = u.reshape(t // cfg.four_n2, cfg.four_n2 * fw)
    f_p = _fourier_group(cfg, u2, 0, cfg.batch, cfg.seq)
    f_s = _fourier_group(cfg, u2, cfg.tokens_p // cfg.four_n2, cfg.dec_batch, cfg.dec_seq)
    mixed = _proj(cfg, attn, f_p, f_s, gates, w_attn_proj[0].astype(BF16), w_four[0].astype(BF16), b_four)
    x1 = _resid(cfg, mixed, w_out[0].astype(BF16), x_p, x_s, ada3)

    hp, idx, wgt, pos, cnt = _router(cfg, x1, norm_ffn, ada3, w_router[0], router_bias[0])
    tme, ne = cfg.tm_expert, cfg.n_experts
    counts = cnt[:, 0]
    padded = (counts + tme - 1) // tme * tme
    pend = jnp.cumsum(padded).astype(I32)
    slot = _slots(cfg, pend - padded, idx, pos)
    n_tiles = t * cfg.top_k // tme + ne
    tile_start = jnp.arange(n_tiles, dtype=I32) * tme
    tile_expert = jnp.minimum(jnp.sum((pend[None, :] <= tile_start[:, None]).astype(I32), axis=1), ne - 1)
    n_used = (pend[-1:] // tme).astype(I32)
    owner = jnp.where(counts > 0, jnp.arange(ne, dtype=I32), ne)
    later = jnp.concatenate([lax.cummin(owner, reverse=True)[1:], jnp.full((1,), ne, I32)])
    next_expert = jnp.where(later < ne, later, -1).astype(I32)
    xs = _dispatch(cfg, hp, slot, pend, padded, n_used, n_tiles * tme)
    ys = _experts(cfg, xs, tile_expert, n_used, next_expert, w_gate_e[0], w_up_e[0], w_down_e[0])

    wgt_t = wgt.T
    shared = (w_gate_s[0].astype(BF16), w_up_s[0].astype(BF16), w_down_s[0].astype(BF16))
    nf = norm_final.reshape(1, d)
    y_p = _final_group(cfg, 0, 0, cfg.batch, cfg.seq, x1, ys, slot, wgt_t, ada3, norm_ffn, *shared, nf)
    y_s = _final_group(cfg, cfg.tokens_p // cfg.tm_final, cfg.batch, cfg.dec_batch, cfg.dec_seq, x1, ys, slot,
                       wgt_t, ada3, norm_ffn, *shared, nf)
    return (y_p.reshape(cfg.batch, cfg.seq, d), y_s.reshape(cfg.dec_batch, cfg.dec_seq, d))


def kernel(x_prompt, x_sample, c_prompt, c_sample, w_ada, b_ada, norm_mix, w_in, rel_bias, sink, w_attn_proj,
           w_four, b_four, w_out, norm_ffn, w_router, router_bias, w_gate_e, w_up_e, w_down_e, w_gate_s, w_up_s,
           w_down_s, norm_final):
    return _forward(Cfg(), x_prompt, x_sample, c_prompt, c_sample, w_ada, b_ada, norm_mix, w_in, rel_bias, sink,
                    w_attn_proj, w_four, b_four, w_out, norm_ffn, w_router, router_bias, w_gate_e, w_up_e,
                    w_down_e, w_gate_s, w_up_s, w_down_s, norm_final)
```

```python
import functools
import math
from typing import NamedTuple

import jax
import jax.numpy as jnp
import numpy as np
from jax import lax
from jax.experimental import pallas as pl
from jax.experimental.pallas import tpu as pltpu

F32 = jnp.float32
BF16 = jnp.bfloat16
I32 = jnp.int32
U32 = jnp.uint32

EPS = 1e-6
NEG_INF = -1e30
LANES = 128
MIB = 1024 * 1024


class Cfg(NamedTuple):
    d_model: int = 2048
    batch: int = 4
    seq: int = 8192
    dec_batch: int = 8
    dec_seq: int = 2048
    n_heads: int = 32
    n_kv: int = 4
    head_dim: int = 64
    window: int = 128
    n_buckets: int = 32
    max_distance: int = 128
    n_fgroups: int = 8
    fgroup: int = 128
    n_experts: int = 256
    top_k: int = 8
    n_egroups: int = 8
    topk_groups: int = 4
    d_expert: int = 512
    d_shared: int = 512
    routed_scale: float = 2.5
    tm_norm: int = 512
    tm_mm: int = 512
    tm_proj: int = 256
    tm_router: int = 256
    tm_dispatch: int = 256
    tm_expert: int = 256
    tm_final: int = 256
    four_n2: int = 128
    four_t2: int = 8

    @property
    def attn_width(self):
        return self.n_heads * self.head_dim

    @property
    def kv_width(self):
        return self.n_kv * self.head_dim

    @property
    def fourier_width(self):
        return self.n_fgroups * self.fgroup

    @property
    def tokens_p(self):
        return self.batch * self.seq

    @property
    def tokens_s(self):
        return self.dec_batch * self.dec_seq

    @property
    def tokens(self):
        return self.tokens_p + self.tokens_s


def _params(vmem_mib):
    return pltpu.CompilerParams(vmem_limit_bytes=int(vmem_mib * MIB))


def _batch_of_tile(i, cfg, tm):
    n_p = cfg.tokens_p // tm
    return jnp.where(i < n_p, i // (cfg.seq // tm), cfg.batch + (i - n_p) // (cfg.dec_seq // tm))


def _ada_spec(cfg, tm, which):
    return pl.BlockSpec((None, 1, cfg.d_model), lambda i: (_batch_of_tile(i, cfg, tm) * 6 + which, 0, 0))


def _two_group_specs(cfg, tm, width):
    n_p = cfg.tokens_p // tm
    return [pl.BlockSpec((tm, width), lambda i: (jnp.minimum(i, n_p - 1), 0)),
            pl.BlockSpec((tm, width), lambda i: (jnp.maximum(i - n_p, 0), 0))]


def _modulated_norm(x, g, shift, scale):
    y = x * lax.rsqrt(jnp.mean(x * x, axis=-1, keepdims=True) + EPS) * g
    return y * (1.0 + scale) + shift


def _pack_bf16_pair(lo, hi):
    lo_bits = pltpu.bitcast(lo.astype(BF16).astype(F32), U32) >> 16
    hi_bits = pltpu.bitcast(hi.astype(BF16).astype(F32), U32) & jnp.uint32(0xFFFF0000)
    return hi_bits | lo_bits


def _unpack_bf16_pair(p):
    lo = pltpu.bitcast(p << 16, F32)
    hi = pltpu.bitcast(p & jnp.uint32(0xFFFF0000), F32)
    return lo, hi


def _pair_block(d):
    return min(256, d // 2)


def _lines(width):
    return width // LANES


def _row_window(row, n_rows, lines):
    return pl.ds(pl.multiple_of(row * lines, lines), n_rows * lines)


def _load_rows(ref, row, n_rows, width):
    lines = _lines(width)
    base = pl.multiple_of(row * lines, lines) if not isinstance(row, int) else row * lines
    return jnp.concatenate([ref[pl.ds(base + c, n_rows, stride=lines), :] for c in range(lines)], axis=1)


def _store_rows(ref, row, mat):
    n_rows, width = mat.shape
    lines = _lines(width)
    for c in range(lines):
        ref[pl.ds(row * lines + c, n_rows, stride=lines), :] = mat[:, c * LANES:(c + 1) * LANES]


def _ada_kernel(c_ref, w_ref, b_ref, o_ref):
    c = c_ref[...]
    s = c * jax.nn.sigmoid(c)
    o_ref[...] = jnp.dot(s, w_ref[...], preferred_element_type=F32, precision=lax.Precision.HIGHEST) + b_ref[...]


def _ada(cfg, c_all, w_ada, b_ada):
    rows, d = c_all.shape
    n = w_ada.shape[1]
    tn = n // 8
    return pl.pallas_call(
        _ada_kernel, grid=(n // tn,),
        in_specs=[pl.BlockSpec((rows, d), lambda j: (0, 0)), pl.BlockSpec((d, tn), lambda j: (0, j)),
                  pl.BlockSpec((1, tn), lambda j: (0, j))],
        out_specs=pl.BlockSpec((rows, tn), lambda j: (0, j)),
        out_shape=jax.ShapeDtypeStruct((rows, n), F32),
        compiler_params=_params(2 * d * tn * 4 / MIB + 8), name="ada")(c_all, w_ada, b_ada)


def _modnorm_kernel(xp_ref, xs_ref, g_ref, sh_ref, sc_ref, o_ref, *, n_p):
    i = pl.program_id(0)

    def run(x_ref):
        o_ref[...] = _modulated_norm(x_ref[...], g_ref[...], sh_ref[...], sc_ref[...]).astype(o_ref.dtype)

    pl.when(i < n_p)(lambda: run(xp_ref))
    pl.when(i >= n_p)(lambda: run(xs_ref))


def _modnorm(cfg, x_p, x_s, g, ada3):
    tm, d = cfg.tm_norm, cfg.d_model
    return pl.pallas_call(
        functools.partial(_modnorm_kernel, n_p=cfg.tokens_p // tm), grid=(cfg.tokens // tm,),
        in_specs=_two_group_specs(cfg, tm, d) + [pl.BlockSpec((1, d), lambda i: (0, 0)),
                                                 _ada_spec(cfg, tm, 0), _ada_spec(cfg, tm, 1)],
        out_specs=pl.BlockSpec((tm, d), lambda i: (i, 0)),
        out_shape=jax.ShapeDtypeStruct((cfg.tokens, d), BF16),
        compiler_params=_params(7 * tm * d * 4 / MIB + 8), name="modnorm")(x_p, x_s, g, ada3, ada3)


def _mm_kernel(a_ref, w_ref, o_ref):
    o_ref[...] = jnp.dot(a_ref[...], w_ref[...], preferred_element_type=F32).astype(o_ref.dtype)


def _matmul(a, w, tm, tn, name):
    m, k = a.shape
    n = w.shape[1]
    vmem = (2 * tm * k * 2 + 2 * k * tn * 2 + 2 * tm * tn * 2 + tm * tn * 4) / MIB + 8
    return pl.pallas_call(
        _mm_kernel, grid=(n // tn, m // tm),
        in_specs=[pl.BlockSpec((tm, k), lambda j, i: (i, 0)), pl.BlockSpec((k, tn), lambda j, i: (0, j))],
        out_specs=pl.BlockSpec((tm, tn), lambda j, i: (i, j)),
        out_shape=jax.ShapeDtypeStruct((m, n), BF16),
        compiler_params=_params(vmem), name=name)(a, w)


def _t5_bucket_table(cfg):
    blk, span = cfg.window, 3 * cfg.window
    rel = (np.arange(span)[None, :] - cfg.window - np.arange(blk)[:, None]).astype(np.int32)
    nb = cfg.n_buckets // 2
    max_exact = nb // 2
    ret = np.where(rel > 0, nb, 0)
    n = np.abs(rel)
    nf = np.maximum(n, 1).astype(np.float32)
    ratio = np.log(nf / np.float32(max_exact)) / np.float32(math.log(cfg.max_distance / max_exact))
    large = max_exact + (ratio * np.float32(nb - max_exact)).astype(np.int32)
    large = np.minimum(large, nb - 1)
    return (ret + np.where(n < max_exact, n, large)).astype(np.int32), rel


def _bias_kernel(bucket_ref, rbt_ref, o_ref, *, n_buckets):
    bucket = bucket_ref[...]
    acc = jnp.zeros(o_ref.shape, F32)
    for b in range(n_buckets):
        acc = jnp.where(bucket == b, rbt_ref[:, b:b + 1], acc)
    o_ref[...] = acc


def _bias_table(cfg, rel_bias):
    buckets, _ = _t5_bucket_table(cfg)
    blk, span, h = cfg.window, 3 * cfg.window, cfg.n_heads
    n = blk * span
    flat = pl.pallas_call(
        functools.partial(_bias_kernel, n_buckets=cfg.n_buckets),
        out_shape=jax.ShapeDtypeStruct((h, n), F32), name="bias")(
            jnp.asarray(buckets.reshape(1, n)), rel_bias.T.astype(F32))
    pairs = h // cfg.n_kv // 2
    t = flat.reshape(cfg.n_kv, pairs, 2, blk, span)
    return jnp.transpose(t, (0, 1, 3, 2, 4)).reshape(cfg.n_kv, pairs * blk, 2 * span)


def _attn_kernel(q_ref, kp_ref, kc_ref, kn_ref, vp_ref, vc_ref, vn_ref, bias_ref, sink_ref, rel_ref, kofs_ref,
                 ones_ref, o_ref, valid_ref, s_ref, p_ref, m_ref, *, cfg, blocks_p):
    blk, hd = cfg.window, cfg.head_dim
    span = 3 * blk
    pairs = cfg.n_heads // cfg.n_kv // 2
    rc = 64
    nb_p, nb_s = cfg.seq // blk, cfg.dec_seq // blk
    i = pl.program_id(0)
    in_p = i < blocks_p
    pos = jnp.where(in_p, i % nb_p, (i - blocks_p) % nb_s)
    seq_len = jnp.where(in_p, cfg.seq, cfg.dec_seq)
    kpos = pos * blk + kofs_ref[...]
    in_band = jnp.abs(rel_ref[...]) <= cfg.window
    in_seq = (kpos >= 0) & (kpos < seq_len)
    valid_ref[...] = jnp.where(in_band, jnp.where(in_seq, 1.0, 0.0), 0.0)

    k = jnp.concatenate([kp_ref[...], kc_ref[...], kn_ref[...]], axis=0)
    v = jnp.concatenate([vp_ref[...], vc_ref[...], vn_ref[...]], axis=0)
    scale = hd ** -0.5
    lane = lax.broadcasted_iota(I32, (pairs * blk, 2 * hd), 1)
    for j in range(cfg.n_kv):
        kj = k[:, j * hd:(j + 1) * hd]
        vj = v[:, j * hd:(j + 1) * hd]
        zero = jnp.zeros_like(kj)
        k2 = jnp.concatenate([jnp.concatenate([kj, zero], axis=1), jnp.concatenate([zero, kj], axis=1)], axis=0)
        v2 = jnp.concatenate([jnp.concatenate([vj, zero], axis=1), jnp.concatenate([zero, vj], axis=1)], axis=0)
        base = j * pairs * 2 * hd
        qs = jnp.concatenate([q_ref[:, base + p * 2 * hd: base + (p + 1) * 2 * hd] for p in range(pairs)], axis=0)
        s_ref[...] = lax.dot_general(qs, k2, (((1,), (1,)), ((), ())), preferred_element_type=F32)
        units = [(slice(r0, r0 + rc), half) for r0 in range(0, pairs * blk, rc) for half in range(2)]

        def masked_logits(rows, half):
            cols = slice(half * span, (half + 1) * span)
            lh = s_ref[rows, cols] * scale + bias_ref[j, rows, cols]
            return jnp.where(valid_ref[rows, cols] > 0.5, lh, NEG_INF)

        for rows, half in units:
            sk = sink_ref[j, rows, half * hd:half * hd + 1]
            m = jnp.maximum(jnp.max(masked_logits(rows, half), axis=-1, keepdims=True), sk)
            m_ref[rows, half * LANES:(half + 1) * LANES] = jnp.broadcast_to(m, (rc, LANES))
        for rows, half in units:
            mb = m_ref[rows, half * LANES:(half + 1) * LANES]
            p = jnp.exp(masked_logits(rows, half) - jnp.concatenate([mb] * (span // LANES), axis=1))
            p_ref[rows, half * span:(half + 1) * span] = p.astype(BF16)
        probs = p_ref[...]
        pv = jnp.dot(probs, v2, preferred_element_type=F32)
        psum = jnp.dot(probs, ones_ref[...], preferred_element_type=F32)
        m_lane = jnp.where(lane < hd, m_ref[:, :LANES], m_ref[:, LANES:])
        out = pv / (psum + jnp.exp(sink_ref[j] - m_lane))
        for p in range(pairs):
            o_ref[:, base + p * 2 * hd: base + (p + 1) * 2 * hd] = out[p * blk:(p + 1) * blk].astype(o_ref.dtype)


def _attention(cfg, qkv, bias_tbl, sink):
    blk, hd = cfg.window, cfg.head_dim
    aw, kvw = cfg.attn_width, cfg.kv_width
    pairs = cfg.n_heads // cfg.n_kv // 2
    nb_p, nb_s = cfg.seq // blk, cfg.dec_seq // blk
    blocks_p = cfg.batch * nb_p
    n_blocks = cfg.tokens // blk
    _, rel = _t5_bucket_table(cfg)
    rel_tbl = np.tile(rel, (pairs, 2)).astype(np.int32)
    kofs_tbl = np.tile((np.arange(3 * blk) - cfg.window)[None, :], (pairs * blk, 2)).astype(np.int32)
    sink_tbl = jnp.broadcast_to(sink.astype(F32).reshape(cfg.n_kv, pairs, 1, 2, 1),
                                (cfg.n_kv, pairs, blk, 2, hd)).reshape(cfg.n_kv, pairs * blk, 2 * hd)
    ones_tbl = jnp.asarray(np.kron(np.eye(2), np.ones((3 * blk, hd))), BF16)

    def seq_pos(i):
        in_p = i < blocks_p
        return jnp.where(in_p, i % nb_p, (i - blocks_p) % nb_s), jnp.where(in_p, nb_p, nb_s)

    def prev_blk(i):
        pos, _ = seq_pos(i)
        return jnp.where(pos == 0, i, i - 1)

    def next_blk(i):
        pos, nb = seq_pos(i)
        return jnp.where(pos == nb - 1, i, i + 1)

    kcol, vcol = aw // kvw, aw // kvw + 1
    full = lambda shape: pl.BlockSpec(shape, lambda i: (0,) * len(shape))
    return pl.pallas_call(
        functools.partial(_attn_kernel, cfg=cfg, blocks_p=blocks_p), grid=(n_blocks,),
        in_specs=[pl.BlockSpec((blk, aw), lambda i: (i, 0)),
                  pl.BlockSpec((blk, kvw), lambda i: (prev_blk(i), kcol)),
                  pl.BlockSpec((blk, kvw), lambda i: (i, kcol)),
                  pl.BlockSpec((blk, kvw), lambda i: (next_blk(i), kcol)),
                  pl.BlockSpec((blk, kvw), lambda i: (prev_blk(i), vcol)),
                  pl.BlockSpec((blk, kvw), lambda i: (i, vcol)),
                  pl.BlockSpec((blk, kvw), lambda i: (next_blk(i), vcol)),
                  full(bias_tbl.shape), full(sink_tbl.shape), full(rel_tbl.shape), full(kofs_tbl.shape),
                  full(ones_tbl.shape)],
        out_specs=pl.BlockSpec((blk, aw), lambda i: (i, 0)),
        out_shape=jax.ShapeDtypeStruct((cfg.tokens, aw), BF16),
        scratch_shapes=[pltpu.VMEM(rel_tbl.shape, F32), pltpu.VMEM(rel_tbl.shape, F32),
                        pltpu.VMEM(rel_tbl.shape, BF16), pltpu.VMEM((pairs * blk, 2 * LANES), F32)],
        compiler_params=_params(40), name="attn")(
            qkv, qkv, qkv, qkv, qkv, qkv, qkv, bias_tbl, sink_tbl, jnp.asarray(rel_tbl), jnp.asarray(kofs_tbl),
            ones_tbl)


def _dft_cos_sin(n):
    ang = 2.0 * np.pi * np.outer(np.arange(n), np.arange(n)) / n
    return np.cos(ang), np.sin(ang)


def _four1_kernel(u_ref, m1_ref, twr_ref, twi_ref, zr_ref, zi_ref, *, n1, t2, width):
    y = jnp.dot(m1_ref[...], u_ref[...], preferred_element_type=F32)
    twr, twi = twr_ref[0], twi_ref[0]
    for l in range(t2):
        sl = slice(l * width, (l + 1) * width)
        yr, yi = y[:n1, sl], y[n1:, sl]
        cr, ci = twr[:, l:l + 1], twi[:, l:l + 1]
        zr_ref[:, sl] = (yr * cr - yi * ci).astype(zr_ref.dtype)
        zi_ref[:, sl] = (yr * ci + yi * cr).astype(zi_ref.dtype)


def _four3_kernel(zr_ref, zi_ref, m3_ref, mc_ref, o_ref, *, n2, n_groups, group, norm):
    z = jnp.concatenate([zr_ref[0], zi_ref[0]], axis=0)
    x = jnp.dot(m3_ref[...], z, preferred_element_type=F32)
    xr, xi = x[:n2].astype(BF16), x[n2:].astype(BF16)
    for g in range(n_groups):
        sl = slice(g * group, (g + 1) * group)
        xg = jnp.concatenate([xr[:, sl], xi[:, sl]], axis=1)
        o_ref[:, sl] = (jnp.dot(xg, mc_ref[...], preferred_element_type=F32) * norm).astype(o_ref.dtype)


def _fourier_group(cfg, u2, row_off, nbatch, seq):
    n2, t2, width = cfg.four_n2, cfg.four_t2, cfg.fourier_width
    n1 = seq // n2
    c1, s1 = _dft_cos_sin(n1)
    m1 = jnp.asarray(np.concatenate([c1, -s1], axis=0), BF16)
    ang = 2.0 * np.pi * np.outer(np.arange(n1), np.arange(n2)) / seq
    twr = jnp.asarray(np.cos(ang).reshape(n1, n2 // t2, t2).transpose(1, 0, 2), F32)
    twi = jnp.asarray((-np.sin(ang)).reshape(n1, n2 // t2, t2).transpose(1, 0, 2), F32)
    blk_off = row_off // n1
    zshape = jax.ShapeDtypeStruct((nbatch * n1, n2 * width), BF16)
    tw_spec = pl.BlockSpec((1, n1, t2), lambda b, j: (j, 0, 0))
    zr, zi = pl.pallas_call(
        functools.partial(_four1_kernel, n1=n1, t2=t2, width=width), grid=(nbatch, n2 // t2),
        in_specs=[pl.BlockSpec((n1, t2 * width), lambda b, j: (blk_off + b, j)),
                  pl.BlockSpec((2 * n1, n1), lambda b, j: (0, 0)), tw_spec, tw_spec],
        out_specs=[pl.BlockSpec((n1, t2 * width), lambda b, j: (b, j))] * 2,
        out_shape=[zshape, zshape],
        compiler_params=_params(32), name="four1")(u2, m1, twr, twi)

    c2, s2 = _dft_cos_sin(n2)
    m3 = jnp.asarray(np.block([[c2, s2], [-s2, c2]]), BF16)
    cc, sc = _dft_cos_sin(cfg.fgroup)
    mc = jnp.asarray(np.concatenate([cc, sc], axis=0), BF16)
    norm = 1.0 / math.sqrt(seq * cfg.fgroup)
    z_spec = pl.BlockSpec((1, n2, width), lambda b, k1: (b * n1 + k1, 0, 0))
    f = pl.pallas_call(
        functools.partial(_four3_kernel, n2=n2, n_groups=cfg.n_fgroups, group=cfg.fgroup, norm=norm),
        grid=(nbatch, n1),
        in_specs=[z_spec, z_spec, pl.BlockSpec((2 * n2, 2 * n2), lambda b, k1: (0, 0)),
                  pl.BlockSpec((2 * cfg.fgroup, cfg.fgroup), lambda b, k1: (0, 0))],
        out_specs=pl.BlockSpec((n2, width), lambda b, k1: (b, k1)),
        out_shape=jax.ShapeDtypeStruct((nbatch * n2, n1 * width), BF16),
        compiler_params=_params(32), name="four3")(
            zr.reshape(nbatch * n1, n2, width), zi.reshape(nbatch * n1, n2, width), m3, mc)
    return f.reshape(nbatch * seq, width)


def _proj_kernel(a_ref, fp_ref, fs_ref, g_ref, wa_ref, wf_ref, bf_ref, o_ref, *, n_p, d):
    i = pl.program_id(0)
    a = jnp.dot(a_ref[...], wa_ref[...], preferred_element_type=F32)

    def run(f_ref):
        f = jnp.dot(f_ref[...], wf_ref[...], preferred_element_type=F32) + bf_ref[...]
        ga = g_ref[:, :d].astype(F32)
        gf = g_ref[:, d:].astype(F32)
        o_ref[...] = (jax.nn.sigmoid(ga) * a + jax.nn.sigmoid(gf) * f).astype(o_ref.dtype)

    pl.when(i < n_p)(lambda: run(fp_ref))
    pl.when(i >= n_p)(lambda: run(fs_ref))


def _proj(cfg, attn, f_p, f_s, gates, wa, wf, bf):
    tm, d, aw, fw = cfg.tm_proj, cfg.d_model, cfg.attn_width, cfg.fourier_width
    const = lambda shape: pl.BlockSpec(shape, lambda i: (0, 0))
    vmem = (2 * (aw + fw) * d * 2 + 2 * tm * (aw + 2 * fw + 3 * d) * 2 + 3 * tm * d * 4) / MIB + 8
    return pl.pallas_call(
        functools.partial(_proj_kernel, n_p=cfg.tokens_p // tm, d=d), grid=(cfg.tokens // tm,),
        in_specs=[pl.BlockSpec((tm, aw), lambda i: (i, 0))] + _two_group_specs(cfg, tm, fw) + [
            pl.BlockSpec((tm, 2 * d), lambda i: (i, 0)), const((aw, d)), const((fw, d)), const((1, d))],
        out_specs=pl.BlockSpec((tm, d), lambda i: (i, 0)),
        out_shape=jax.ShapeDtypeStruct((cfg.tokens, d), BF16),
        compiler_params=_params(vmem), name="proj")(attn, f_p, f_s, gates, wa, wf, bf)


def _resid_kernel(m_ref, wo_ref, xp_ref, xs_ref, gate_ref, o_ref, *, n_p):
    i = pl.program_id(0)
    out = jnp.dot(m_ref[...], wo_ref[...], preferred_element_type=F32)

    def run(x_ref):
        o_ref[...] = x_ref[...] + gate_ref[...] * out

    pl.when(i < n_p)(lambda: run(xp_ref))
    pl.when(i >= n_p)(lambda: run(xs_ref))


def _resid(cfg, mixed, wo, x_p, x_s, ada3):
    tm, d = cfg.tm_proj, cfg.d_model
    vmem = (2 * d * d * 2 + 2 * tm * d * 2 + 7 * tm * d * 4) / MIB + 8
    return pl.pallas_call(
        functools.partial(_resid_kernel, n_p=cfg.tokens_p // tm), grid=(cfg.tokens // tm,),
        in_specs=[pl.BlockSpec((tm, d), lambda i: (i, 0)), pl.BlockSpec((d, d), lambda i: (0, 0))]
        + _two_group_specs(cfg, tm, d) + [_ada_spec(cfg, tm, 2)],
        out_specs=pl.BlockSpec((tm, d), lambda i: (i, 0)),
        out_shape=jax.ShapeDtypeStruct((cfg.tokens, d), F32),
        compiler_params=_params(vmem), name="resid")(mixed, wo, x_p, x_s, ada3)


def _router_kernel(x_ref, g_ref, sh_ref, sc_ref, wrt_ref, rb_ref, tri_ref,
                   hp_ref, idx_ref, wgt_ref, pos_ref, cnt_ref, carry_ref, *, cfg):
    d, ne, ng = cfg.d_model, cfg.n_experts, cfg.n_egroups
    per = ne // ng
    i = pl.program_id(0)

    @pl.when(i == 0)
    def _():
        carry_ref[...] = jnp.zeros_like(carry_ref)

    h = _modulated_norm(x_ref[...], g_ref[...], sh_ref[...], sc_ref[...])
    tm = h.shape[0]
    _store_rows(hp_ref, 0, _pack_bf16_pair(h[:, :d // 2], h[:, d // 2:]))

    logits = lax.dot_general(wrt_ref[...], h, (((1,), (1,)), ((), ())), preferred_element_type=F32,
                             precision=lax.Precision.HIGHEST)
    scores = jax.nn.sigmoid(logits)
    sel = scores + rb_ref[...]
    s3 = sel.reshape(ng, per, tm)
    io3 = lax.broadcasted_iota(I32, (ng, per, tm), 1)
    m1 = jnp.max(s3, axis=1, keepdims=True)
    i1 = jnp.min(jnp.where(s3 == m1, io3, per), axis=1, keepdims=True)
    m2 = jnp.max(jnp.where(io3 == i1, -jnp.inf, s3), axis=1, keepdims=True)
    gs = (m1 + m2).reshape(ng, tm)
    gio = lax.broadcasted_iota(I32, (ng, tm), 0)
    before = jnp.zeros((ng, tm), I32)
    for g2 in range(ng):
        row = gs[g2:g2 + 1, :]
        ahead = jnp.where(row > gs, 1, jnp.where(row == gs, jnp.where(gio > g2, 1, 0), 0))
        before = before + ahead
    keep = jnp.where(before < cfg.topk_groups, 1.0, 0.0)
    keep_e = jnp.broadcast_to(keep.reshape(ng, 1, tm), (ng, per, tm)).reshape(ne, tm)
    cur = jnp.where(keep_e > 0.5, sel, -jnp.inf)

    eio = lax.broadcasted_iota(I32, (ne, tm), 0)
    chosen, picked = [], jnp.zeros((ne, tm), F32)
    for _ in range(cfg.top_k):
        m = jnp.max(cur, axis=0, keepdims=True)
        ij = jnp.min(jnp.where(cur == m, eio, ne), axis=0, keepdims=True)
        hit = eio == ij
        picked = jnp.where(hit, 1.0, picked)
        cur = jnp.where(hit, -jnp.inf, cur)
        chosen.append(ij)
    before_t = jnp.dot(picked.astype(BF16), tri_ref[...], preferred_element_type=F32)
    rank = carry_ref[...] + before_t
    sc_rows, pos_rows = [], []
    for ij in chosen:
        hit = eio == ij
        sc_rows.append(jnp.sum(jnp.where(hit, scores, 0.0), axis=0, keepdims=True))
        pos_rows.append(jnp.sum(jnp.where(hit, rank, 0.0), axis=0, keepdims=True))
    sc_all = jnp.concatenate(sc_rows, axis=0)
    total = jnp.sum(sc_all, axis=0, keepdims=True)
    idx_ref[...] = jnp.concatenate(chosen, axis=0)
    wgt_ref[...] = sc_all / total * cfg.routed_scale
    pos_ref[...] = jnp.concatenate(pos_rows, axis=0).astype(I32)
    carry_ref[...] = carry_ref[...] + jnp.sum(picked, axis=1, keepdims=True)
    cnt_ref[...] = carry_ref[...].astype(I32)


def _router(cfg, x1, g, ada3, w_router, router_bias):
    tm, d, ne, k = cfg.tm_router, cfg.d_model, cfg.n_experts, cfg.top_k
    t = cfg.tokens
    tri = jnp.asarray(np.triu(np.ones((tm, tm), np.float32), 1), BF16)
    row = lambda dt: jax.ShapeDtypeStruct((k, t), dt)
    row_spec = pl.BlockSpec((k, tm), lambda i: (0, i))
    return pl.pallas_call(
        functools.partial(_router_kernel, cfg=cfg), grid=(t // tm,),
        in_specs=[pl.BlockSpec((tm, d), lambda i: (i, 0)), pl.BlockSpec((1, d), lambda i: (0, 0)),
                  _ada_spec(cfg, tm, 3), _ada_spec(cfg, tm, 4),
                  pl.BlockSpec((ne, d), lambda i: (0, 0)), pl.BlockSpec((ne, 1), lambda i: (0, 0)),
                  pl.BlockSpec((tm, tm), lambda i: (0, 0))],
        out_specs=[pl.BlockSpec((tm * _lines(d // 2), LANES), lambda i: (i, 0)), row_spec, row_spec, row_spec,
                   pl.BlockSpec((ne, 1), lambda i: (0, 0))],
        out_shape=[jax.ShapeDtypeStruct((t * _lines(d // 2), LANES), U32), row(I32), row(F32), row(I32),
                   jax.ShapeDtypeStruct((ne, 1), I32)],
        scratch_shapes=[pltpu.VMEM((ne, 1), F32)],
        compiler_params=_params(40), name="router")(
            x1, g, ada3, ada3, w_router.T.astype(F32), router_bias.reshape(ne, 1).astype(F32), tri)


def _slots_kernel(pstart_ref, idx_ref, pos_ref, o_ref, *, n_experts):
    idx = idx_ref[...]
    base = lax.fori_loop(0, n_experts, lambda e, acc: jnp.where(idx == e, pstart_ref[e], acc),
                         jnp.zeros(idx.shape, I32))
    o_ref[...] = base + pos_ref[...]


def _slots(cfg, pstart, idx, pos):
    k, t = idx.shape
    tn = math.gcd(t, 4096)
    spec = pl.BlockSpec((k, tn), lambda i, *_: (0, i))
    grid_spec = pltpu.PrefetchScalarGridSpec(num_scalar_prefetch=1, grid=(t // tn,), in_specs=[spec, spec],
                                             out_specs=spec)
    return pl.pallas_call(functools.partial(_slots_kernel, n_experts=cfg.n_experts), grid_spec=grid_spec,
                          out_shape=jax.ShapeDtypeStruct((k, t), I32), name="slots")(pstart, idx, pos)


def _dispatch_kernel(pend_ref, padded_ref, nt_ref, hp_ref, slot_ref, xs_ref, zero_ref, sem, zsem, *, cfg):
    tm, tme, k, ne = cfg.tm_dispatch, cfg.tm_expert, cfg.top_k, cfg.n_experts
    lines = _lines(cfg.d_model // 2)
    n_tiles = xs_ref.shape[0] // (tme * lines)
    i = pl.program_id(0)

    def pad_copy(e):
        return pltpu.make_async_copy(zero_ref, xs_ref.at[_row_window(pend_ref[e] - tme, tme, lines)], zsem)

    def tail_copy(t):
        return pltpu.make_async_copy(zero_ref, xs_ref.at[_row_window(t * tme, tme, lines)], zsem)

    @pl.when(i == 0)
    def _():
        zero_ref[...] = jnp.zeros_like(zero_ref)

        def start(e, c):
            pl.when(padded_ref[e] > 0)(lambda: pad_copy(e).start())
            return c

        def wait(e, c):
            pl.when(padded_ref[e] > 0)(lambda: pad_copy(e).wait())
            return c

        lax.fori_loop(0, ne, start, 0)
        lax.fori_loop(nt_ref[0], n_tiles, lambda t, c: (tail_copy(t).start(), c)[1], 0)
        lax.fori_loop(0, ne, wait, 0)
        lax.fori_loop(nt_ref[0], n_tiles, lambda t, c: (tail_copy(t).wait(), c)[1], 0)

    def row(r, c):
        for j in range(k):
            pltpu.make_async_copy(hp_ref.at[_row_window(r, 1, lines)],
                                  xs_ref.at[_row_window(slot_ref[j, r], 1, lines)], sem).start(priority=j % 2)
        return c

    lax.fori_loop(0, tm, row, 0)
    for _ in range(k):
        pltpu.make_async_copy(hp_ref, xs_ref.at[_row_window(0, tm, lines)], sem).wait()


def _dispatch(cfg, hp, slot, pend, padded, n_used, n_slots):
    tm, d, k = cfg.tm_dispatch, cfg.d_model, cfg.top_k
    lines = _lines(d // 2)
    grid_spec = pltpu.PrefetchScalarGridSpec(
        num_scalar_prefetch=3, grid=(cfg.tokens // tm,),
        in_specs=[pl.BlockSpec((tm * lines, LANES), lambda i, *_: (i, 0)),
                  pl.BlockSpec((k, tm), lambda i, *_: (0, i), memory_space=pltpu.SMEM)],
        out_specs=pl.BlockSpec(memory_space=pl.ANY),
        scratch_shapes=[pltpu.VMEM((cfg.tm_expert * lines, LANES), U32), pltpu.SemaphoreType.DMA,
                        pltpu.SemaphoreType.DMA])
    return pl.pallas_call(
        functools.partial(_dispatch_kernel, cfg=cfg), grid_spec=grid_spec,
        out_shape=jax.ShapeDtypeStruct((n_slots * lines, LANES), U32),
        compiler_params=_params(24), name="dispatch")(pend, padded, n_used, hp, slot)


def _expert_kernel(te_ref, nt_ref, nxt_ref, xs_ref, wg_hbm, wu_hbm, wd_hbm, ys_ref,
                   wgf, wuf, wdf, wgb, wub, wdb, side_ref, wsem):
    i = pl.program_id(0)
    nt = nt_ref[0]
    tme = xs_ref.shape[0] // _lines(wdb.shape[1] // 2)
    half = wdb.shape[1] // 2

    def weight_copies(e, s):
        return [pltpu.make_async_copy(src.at[e], dst.at[s], wsem.at[s])
                for src, dst in ((wg_hbm, wgf), (wu_hbm, wuf), (wd_hbm, wdf))]

    @pl.when(i >= nt)
    def _():
        ys_ref[...] = jnp.zeros_like(ys_ref)

    @pl.when(i < nt)
    def _():
        e = te_ref[i]

        @pl.when(i == 0)
        def _():
            side_ref[0] = 0
            for c in weight_copies(e, 0):
                c.start()

        @pl.when(jnp.logical_or(i == 0, e != te_ref[jnp.maximum(i - 1, 0)]))
        def _():
            s = side_ref[0]
            nxt = nxt_ref[e]
            @pl.when(nxt >= 0)
            def _():
                for c in weight_copies(nxt, 1 - s):
                    c.start()

            for c in weight_copies(e, s):
                c.wait()
            wgb[...] = wgf[s].astype(BF16)
            wub[...] = wuf[s].astype(BF16)
            wdb[...] = wdf[s].astype(BF16)
            side_ref[0] = 1 - s

        lo, hi = _unpack_bf16_pair(_load_rows(xs_ref, 0, tme, half))
        x = jnp.concatenate([lo.astype(BF16), hi.astype(BF16)], axis=1)
        g = jnp.dot(x, wgb[...], preferred_element_type=F32)
        u = jnp.dot(x, wub[...], preferred_element_type=F32)
        a = (g * jax.nn.sigmoid(g) * u).astype(BF16)
        cb = _pair_block(2 * half)
        lines = _lines(half)
        for m in range(half // cb):
            ym = jnp.dot(a, wdb[:, 2 * m * cb:(2 * m + 2) * cb], preferred_element_type=F32)
            packed = _pack_bf16_pair(ym[:, :cb], ym[:, cb:])
            for c in range(cb // LANES):
                ys_ref[pl.ds(m * (cb // LANES) + c, tme, stride=lines), :] = packed[:, c * LANES:(c + 1) * LANES]


def _experts(cfg, xs, tile_expert, n_used, next_expert, wg, wu, wd):
    tme, d, de = cfg.tm_expert, cfg.d_model, cfg.d_expert
    lines = _lines(d // 2)
    n_tiles = xs.shape[0] // (tme * lines)
    any_spec = pl.BlockSpec(memory_space=pl.ANY)
    grid_spec = pltpu.PrefetchScalarGridSpec(
        num_scalar_prefetch=3, grid=(n_tiles,),
        in_specs=[pl.BlockSpec((tme * lines, LANES), lambda i, te, nt, nx: (jnp.minimum(i, nt[0] - 1), 0)),
                  any_spec, any_spec, any_spec],
        out_specs=pl.BlockSpec((tme * lines, LANES), lambda i, te, nt, nx: (i, 0)),
        scratch_shapes=[pltpu.VMEM((2, d, de), F32), pltpu.VMEM((2, d, de), F32), pltpu.VMEM((2, de, d), F32),
                        pltpu.VMEM((d, de), BF16), pltpu.VMEM((d, de), BF16), pltpu.VMEM((de, d), BF16),
                        pltpu.SMEM((1,), I32), pltpu.SemaphoreType.DMA((2,))])
    vmem = (2 * 3 * d * de * 4 + 3 * d * de * 2 + 6 * tme * d * 2 + 6 * tme * d * 4) / MIB + 8
    return pl.pallas_call(
        _expert_kernel, grid_spec=grid_spec, out_shape=jax.ShapeDtypeStruct(xs.shape, U32),
        compiler_params=_params(vmem), name="experts")(tile_expert, n_used, next_expert, xs, wg, wu, wd)


def _final_kernel(slot_ref, slotn_ref, x_ref, wt_ref, g_ref, sh_ref, sc_ref, gate_ref, wg_ref, wu_ref, wd_ref,
                  nf_ref, ys_ref, o_ref, buf, sem, *, cfg):
    tm, k = cfg.tm_final, cfg.top_k
    half = cfg.d_model // 2
    lines = _lines(half)
    i = pl.program_id(0)
    n = pl.num_programs(0)

    def plane(b, j):
        return (b * k + j) * tm

    def gather(slots, b):
        def row(r, c):
            for j in range(k):
                pltpu.make_async_copy(ys_ref.at[_row_window(slots[j, r], 1, lines)],
                                      buf.at[_row_window(plane(b, j) + r, 1, lines)],
                                      sem.at[b]).start(priority=j % 2)
            return c

        lax.fori_loop(0, tm, row, 0)

    pl.when(i == 0)(lambda: gather(slot_ref, 0))
    pl.when(i + 1 < n)(lambda: gather(slotn_ref, (i + 1) % 2))
    b = i % 2
    for j in range(k):
        pltpu.make_async_copy(ys_ref.at[_row_window(0, tm, lines)], buf.at[_row_window(plane(b, j), tm, lines)],
                              sem.at[b]).wait()

    x = x_ref[...]
    h = _modulated_norm(x, g_ref[...], sh_ref[...], sc_ref[...]).astype(BF16)
    g = jnp.dot(h, wg_ref[...], preferred_element_type=F32)
    u = jnp.dot(h, wu_ref[...], preferred_element_type=F32)
    y = jnp.dot((g * jax.nn.sigmoid(g) * u).astype(BF16), wd_ref[...], preferred_element_type=F32)
    cb = _pair_block(2 * half)
    for j in range(k):
        lo, hi = _unpack_bf16_pair(_load_rows(buf, plane(b, j), tm, half))
        blocks = [part[:, m * cb:(m + 1) * cb] for m in range(half // cb) for part in (lo, hi)]
        y = y + wt_ref[:, j:j + 1] * jnp.concatenate(blocks, axis=1)
    xo = x + gate_ref[...] * y
    o_ref[...] = xo * lax.rsqrt(jnp.mean(xo * xo, axis=-1, keepdims=True) + EPS) * nf_ref[...]


def _final_group(cfg, tile_off, batch_off, nbatch, seq, x1, ys, slot, wgt_t, ada3, g, wgs, wus, wds, nf):
    tm, d, k, ds = cfg.tm_final, cfg.d_model, cfg.top_k, cfg.d_shared
    n = nbatch * seq // tm
    per_batch = seq // tm
    ada = lambda which: pl.BlockSpec((None, 1, d), lambda i: ((batch_off + i // per_batch) * 6 + which, 0, 0))
    const = lambda shape: pl.BlockSpec(shape, lambda i: (0, 0))
    vmem = (2 * k * tm * d * 2 + 2 * 3 * d * ds * 2 + 10 * tm * d * 4) / MIB + 8
    return pl.pallas_call(
        functools.partial(_final_kernel, cfg=cfg), grid=(n,),
        in_specs=[pl.BlockSpec((k, tm), lambda i: (0, tile_off + i), memory_space=pltpu.SMEM),
                  pl.BlockSpec((k, tm), lambda i: (0, tile_off + jnp.minimum(i + 1, n - 1)), memory_space=pltpu.SMEM),
                  pl.BlockSpec((tm, d), lambda i: (tile_off + i, 0)),
                  pl.BlockSpec((tm, k), lambda i: (tile_off + i, 0)),
                  const((1, d)), ada(3), ada(4), ada(5), const((d, ds)), const((d, ds)), const((ds, d)),
                  const((1, d)), pl.BlockSpec(memory_space=pl.ANY)],
        out_specs=pl.BlockSpec((tm, d), lambda i: (i, 0)),
        out_shape=jax.ShapeDtypeStruct((nbatch * seq, d), F32),
        scratch_shapes=[pltpu.VMEM((2 * k * tm * _lines(d // 2), LANES), U32), pltpu.SemaphoreType.DMA((2,))],
        compiler_params=_params(vmem), name="final")(
            slot, slot, x1, wgt_t, g, ada3, ada3, ada3, wgs, wus, wds, nf, ys)


def _forward(cfg, x_prompt, x_sample, c_prompt, c_sample, w_ada, b_ada, norm_mix, w_in, rel_bias, sink,
             w_attn_proj, w_four, b_four, w_out, norm_ffn, w_router, router_bias, w_gate_e, w_up_e, w_down_e,
             w_gate_s, w_up_s, w_down_s, norm_final):
    d, t = cfg.d_model, cfg.tokens
    aw, kvw, fw = cfg.attn_width, cfg.kv_width, cfg.fourier_width
    x_p = x_prompt.reshape(cfg.tokens_p, d)
    x_s = x_sample.reshape(cfg.tokens_s, d)

    nc = cfg.batch + cfg.dec_batch
    rows = -(-nc // 8) * 8
    c_all = jnp.concatenate([c_prompt, c_sample, jnp.zeros((rows - nc, d), F32)], axis=0)
    ada3 = _ada(cfg, c_all, w_ada[0], b_ada).reshape(rows * 6, 1, d)

    h = _modnorm(cfg, x_p, x_s, norm_mix, ada3)
    w_in_b = w_in[0].astype(BF16)
    c0, c1 = aw + 2 * kvw, aw + 2 * kvw + fw
    qkv = _matmul(h, w_in_b[:, :c0], cfg.tm_mm, c0, "mm_qkv")
    u = _matmul(h, w_in_b[:, c0:c1], cfg.tm_mm, fw, "mm_u")
    gates = _matmul(h, w_in_b[:, c1:], cfg.tm_mm, d, "mm_gates")

    attn = _attention(cfg, qkv, _bias_table(cfg, rel_bias), sink[0])
    u2 = u.reshape(t // cfg.four_n2, cfg.four_n2 * fw)
    f_p = _fourier_group(cfg, u2, 0, cfg.batch, cfg.seq)
    f_s = _fourier_group(cfg, u2, cfg.tokens_p // cfg.four_n2, cfg.dec_batch, cfg.dec_seq)
    mixed = _proj(cfg, attn, f_p, f_s, gates, w_attn_proj[0].astype(BF16), w_four[0].astype(BF16), b_four)
    x1 = _resid(cfg, mixed, w_out[0].astype(BF16), x_p, x_s, ada3)

    hp, idx, wgt, pos, cnt = _router(cfg, x1, norm_ffn, ada3, w_router[0], router_bias[0])
    tme, ne = cfg.tm_expert, cfg.n_experts
    counts = cnt[:, 0]
    padded = (counts + tme - 1) // tme * tme
    pend = jnp.cumsum(padded).astype(I32)
    slot = _slots(cfg, pend - padded, idx, pos)
    n_tiles = t * cfg.top_k // tme + ne
    tile_start = jnp.arange(n_tiles, dtype=I32) * tme
    tile_expert = jnp.minimum(jnp.sum((pend[None, :] <= tile_start[:, None]).astype(I32), axis=1), ne - 1)
    n_used = (pend[-1:] // tme).astype(I32)
    owner = jnp.where(counts > 0, jnp.arange(ne, dtype=I32), ne)
    later = jnp.concatenate([lax.cummin(owner, reverse=True)[1:], jnp.full((1,), ne, I32)])
    next_expert = jnp.where(later < ne, later, -1).astype(I32)
    xs = _dispatch(cfg, hp, slot, pend, padded, n_used, n_tiles * tme)
    ys = _experts(cfg, xs, tile_expert, n_used, next_expert, w_gate_e[0], w_up_e[0], w_down_e[0])

    wgt_t = wgt.T
    shared = (w_gate_s[0].astype(BF16), w_up_s[0].astype(BF16), w_down_s[0].astype(BF16))
    nf = norm_final.reshape(1, d)
    y_p = _final_group(cfg, 0, 0, cfg.batch, cfg.seq, x1, ys, slot, wgt_t, ada3, norm_ffn, *shared, nf)
    y_s = _final_group(cfg, cfg.tokens_p // cfg.tm_final, cfg.batch, cfg.dec_batch, cfg.dec_seq, x1, ys, slot,
                       wgt_t, ada3, norm_ffn, *shared, nf)
    return (y_p.reshape(cfg.batch, cfg.seq, d), y_s.reshape(cfg.dec_batch, cfg.dec_seq, d))


def kernel(x_prompt, x_sample, c_prompt, c_sample, w_ada, b_ada, norm_mix, w_in, rel_bias, sink, w_attn_proj,
           w_four, b_four, w_out, norm_ffn, w_router, router_bias, w_gate_e, w_up_e, w_down_e, w_gate_s, w_up_s,
           w_down_s, norm_final):
    return _forward(Cfg(), x_prompt, x_sample, c_prompt, c_sample, w_ada, b_ada, norm_mix, w_in, rel_bias, sink,
                    w_attn_proj, w_four, b_four, w_out, norm_ffn, w_router, router_bias, w_gate_e, w_up_e,
                    w_down_e, w_gate_s, w_up_s, w_down_s, norm_final)
```

```python
import functools
import math
from typing import NamedTuple

import jax
import jax.numpy as jnp
import numpy as np
from jax import lax
from jax.experimental import pallas as pl
from jax.experimental.pallas import tpu as pltpu

F32 = jnp.float32
BF16 = jnp.bfloat16
I32 = jnp.int32
U32 = jnp.uint32

EPS = 1e-6
NEG_INF = -1e30
LANES = 128
MIB = 1024 * 1024


class Cfg(NamedTuple):
    d_model: int = 2048
    batch: int = 4
    seq: int = 8192
    dec_batch: int = 8
    dec_seq: int = 2048
    n_heads: int = 32
    n_kv: int = 4
    head_dim: int = 64
    window: int = 128
    n_buckets: int = 32
    max_distance: int = 128
    n_fgroups: int = 8
    fgroup: int = 128
    n_experts: int = 256
    top_k: int = 8
    n_egroups: int = 8
    topk_groups: int = 4
    d_expert: int = 512
    d_shared: int = 512
    routed_scale: float = 2.5
    tm_norm: int = 512
    tm_mm: int = 512
    tm_proj: int = 256
    tm_router: int = 256
    tm_dispatch: int = 256
    tm_expert: int = 256
    tm_final: int = 256
    four_n2: int = 128
    four_t2: int = 8

    @property
    def attn_width(self):
        return self.n_heads * self.head_dim

    @property
    def kv_width(self):
        return self.n_kv * self.head_dim

    @property
    def fourier_width(self):
        return self.n_fgroups * self.fgroup

    @property
    def tokens_p(self):
        return self.batch * self.seq

    @property
    def tokens_s(self):
        return self.dec_batch * self.dec_seq

    @property
    def tokens(self):
        return self.tokens_p + self.tokens_s


def _params(vmem_mib):
    return pltpu.CompilerParams(vmem_limit_bytes=int(vmem_mib * MIB))


def _batch_of_tile(i, cfg, tm):
    n_p = cfg.tokens_p // tm
    return jnp.where(i < n_p, i // (cfg.seq // tm), cfg.batch + (i - n_p) // (cfg.dec_seq // tm))


def _ada_spec(cfg, tm, which):
    return pl.BlockSpec((None, 1, cfg.d_model), lambda i: (_batch_of_tile(i, cfg, tm) * 6 + which, 0, 0))


def _two_group_specs(cfg, tm, width):
    n_p = cfg.tokens_p // tm
    return [pl.BlockSpec((tm, width), lambda i: (jnp.minimum(i, n_p - 1), 0)),
            pl.BlockSpec((tm, width), lambda i: (jnp.maximum(i - n_p, 0), 0))]


def _modulated_norm(x, g, shift, scale):
    y = x * lax.rsqrt(jnp.mean(x * x, axis=-1, keepdims=True) + EPS) * g
    return y * (1.0 + scale) + shift


def _pack_bf16_pair(lo, hi):
    lo_bits = pltpu.bitcast(lo.astype(BF16).astype(F32), U32) >> 16
    hi_bits = pltpu.bitcast(hi.astype(BF16).astype(F32), U32) & jnp.uint32(0xFFFF0000)
    return hi_bits | lo_bits


def _unpack_bf16_pair(p):
    lo = pltpu.bitcast(p << 16, F32)
    hi = pltpu.bitcast(p & jnp.uint32(0xFFFF0000), F32)
    return lo, hi


def _pair_block(d):
    return min(256, d // 2)


def _lines(width):
    return width // LANES


def _row_window(row, n_rows, lines):
    return pl.ds(pl.multiple_of(row * lines, lines), n_rows * lines)


def _load_rows(ref, row, n_rows, width):
    lines = _lines(width)
    base = pl.multiple_of(row * lines, lines) if not isinstance(row, int) else row * lines
    return jnp.concatenate([ref[pl.ds(base + c, n_rows, stride=lines), :] for c in range(lines)], axis=1)


def _store_rows(ref, row, mat):
    n_rows, width = mat.shape
    lines = _lines(width)
    for c in range(lines):
        ref[pl.ds(row * lines + c, n_rows, stride=lines), :] = mat[:, c * LANES:(c + 1) * LANES]


def _ada_kernel(c_ref, w_ref, b_ref, o_ref):
    c = c_ref[...]
    s = c * jax.nn.sigmoid(c)
    o_ref[...] = jnp.dot(s, w_ref[...], preferred_element_type=F32, precision=lax.Precision.HIGHEST) + b_ref[...]


def _ada(cfg, c_all, w_ada, b_ada):
    rows, d = c_all.shape
    n = w_ada.shape[1]
    tn = n // 8
    return pl.pallas_call(
        _ada_kernel, grid=(n // tn,),
        in_specs=[pl.BlockSpec((rows, d), lambda j: (0, 0)), pl.BlockSpec((d, tn), lambda j: (0, j)),
                  pl.BlockSpec((1, tn), lambda j: (0, j))],
        out_specs=pl.BlockSpec((rows, tn), lambda j: (0, j)),
        out_shape=jax.ShapeDtypeStruct((rows, n), F32),
        compiler_params=_params(2 * d * tn * 4 / MIB + 8), name="ada")(c_all, w_ada, b_ada)


def _modnorm_kernel(xp_ref, xs_ref, g_ref, sh_ref, sc_ref, o_ref, *, n_p):
    i = pl.program_id(0)

    def run(x_ref):
        o_ref[...] = _modulated_norm(x_ref[...], g_ref[...], sh_ref[...], sc_ref[...]).astype(o_ref.dtype)

    pl.when(i < n_p)(lambda: run(xp_ref))
    pl.when(i >= n_p)(lambda: run(xs_ref))


def _modnorm(cfg, x_p, x_s, g, ada3):
    tm, d = cfg.tm_norm, cfg.d_model
    return pl.pallas_call(
        functools.partial(_modnorm_kernel, n_p=cfg.tokens_p // tm), grid=(cfg.tokens // tm,),
        in_specs=_two_group_specs(cfg, tm, d) + [pl.BlockSpec((1, d), lambda i: (0, 0)),
                                                 _ada_spec(cfg, tm, 0), _ada_spec(cfg, tm, 1)],
        out_specs=pl.BlockSpec((tm, d), lambda i: (i, 0)),
        out_shape=jax.ShapeDtypeStruct((cfg.tokens, d), BF16),
        compiler_params=_params(7 * tm * d * 4 / MIB + 8), name="modnorm")(x_p, x_s, g, ada3, ada3)


def _mm_kernel(a_ref, w_ref, o_ref):
    o_ref[...] = jnp.dot(a_ref[...], w_ref[...], preferred_element_type=F32).astype(o_ref.dtype)


def _matmul(a, w, tm, tn, name):
    m, k = a.shape
    n = w.shape[1]
    vmem = (2 * tm * k * 2 + 2 * k * tn * 2 + 2 * tm * tn * 2 + tm * tn * 4) / MIB + 8
    return pl.pallas_call(
        _mm_kernel, grid=(n // tn, m // tm),
        in_specs=[pl.BlockSpec((tm, k), lambda j, i: (i, 0)), pl.BlockSpec((k, tn), lambda j, i: (0, j))],
        out_specs=pl.BlockSpec((tm, tn), lambda j, i: (i, j)),
        out_shape=jax.ShapeDtypeStruct((m, n), BF16),
        compiler_params=_params(vmem), name=name)(a, w)


def _t5_bucket_table(cfg):
    blk, span = cfg.window, 3 * cfg.window
    rel = (np.arange(span)[None, :] - cfg.window - np.arange(blk)[:, None]).astype(np.int32)
    nb = cfg.n_buckets // 2
    max_exact = nb // 2
    ret = np.where(rel > 0, nb, 0)
    n = np.abs(rel)
    nf = np.maximum(n, 1).astype(np.float32)
    ratio = np.log(nf / np.float32(max_exact)) / np.float32(math.log(cfg.max_distance / max_exact))
    large = max_exact + (ratio * np.float32(nb - max_exact)).astype(np.int32)
    large = np.minimum(large, nb - 1)
    return (ret + np.where(n < max_exact, n, large)).astype(np.int32), rel


def _bias_kernel(bucket_ref, rbt_ref, o_ref, *, n_buckets):
    bucket = bucket_ref[...]
    acc = jnp.zeros(o_ref.shape, F32)
    for b in range(n_buckets):
        acc = jnp.where(bucket == b, rbt_ref[:, b:b + 1], acc)
    o_ref[...] = acc


def _bias_table(cfg, rel_bias):
    buckets, _ = _t5_bucket_table(cfg)
    blk, span, h = cfg.window, 3 * cfg.window, cfg.n_heads
    n = blk * span
    flat = pl.pallas_call(
        functools.partial(_bias_kernel, n_buckets=cfg.n_buckets),
        out_shape=jax.ShapeDtypeStruct((h, n), F32), name="bias")(
            jnp.asarray(buckets.reshape(1, n)), rel_bias.T.astype(F32))
    pairs = h // cfg.n_kv // 2
    t = flat.reshape(cfg.n_kv, pairs, 2, blk, span)
    return jnp.transpose(t, (0, 1, 3, 2, 4)).reshape(cfg.n_kv, pairs * blk, 2 * span)


def _attn_kernel(q_ref, kp_ref, kc_ref, kn_ref, vp_ref, vc_ref, vn_ref, bias_ref, sink_ref, rel_ref, kofs_ref,
                 ones_ref, o_ref, valid_ref, s_ref, l_ref, p_ref, m_ref, *, cfg, blocks_p):
    blk, hd = cfg.window, cfg.head_dim
    span = 3 * blk
    pairs = cfg.n_heads // cfg.n_kv // 2
    rc = 64
    nb_p, nb_s = cfg.seq // blk, cfg.dec_seq // blk
    i = pl.program_id(0)
    in_p = i < blocks_p
    pos = jnp.where(in_p, i % nb_p, (i - blocks_p) % nb_s)
    seq_len = jnp.where(in_p, cfg.seq, cfg.dec_seq)
    kpos = pos * blk + kofs_ref[...]
    in_band = jnp.abs(rel_ref[...]) <= cfg.window
    in_seq = (kpos >= 0) & (kpos < seq_len)
    valid_ref[...] = jnp.where(in_band, jnp.where(in_seq, 1.0, 0.0), 0.0)

    k = jnp.concatenate([kp_ref[...], kc_ref[...], kn_ref[...]], axis=0)
    v = jnp.concatenate([vp_ref[...], vc_ref[...], vn_ref[...]], axis=0)
    scale = hd ** -0.5
    lane = lax.broadcasted_iota(I32, (pairs * blk, 2 * hd), 1)
    for j in range(cfg.n_kv):
        kj = k[:, j * hd:(j + 1) * hd]
        vj = v[:, j * hd:(j + 1) * hd]
        zero = jnp.zeros_like(kj)
        k2 = jnp.concatenate([jnp.concatenate([kj, zero], axis=1), jnp.concatenate([zero, kj], axis=1)], axis=0)
        v2 = jnp.concatenate([jnp.concatenate([vj, zero], axis=1), jnp.concatenate([zero, vj], axis=1)], axis=0)
        base = j * pairs * 2 * hd
        qs = jnp.concatenate([q_ref[:, base + p * 2 * hd: base + (p + 1) * 2 * hd] for p in range(pairs)], axis=0)
        s_ref[...] = lax.dot_general(qs, k2, (((1,), (1,)), ((), ())), preferred_element_type=F32)
        units = [(slice(r0, r0 + rc), half) for r0 in range(0, pairs * blk, rc) for half in range(2)]

        for rows, half in units:
            cols = slice(half * span, (half + 1) * span)
            lh = s_ref[rows, cols] * scale + bias_ref[j, rows, cols]
            lh = jnp.where(valid_ref[rows, cols] > 0.5, lh, NEG_INF)
            l_ref[rows, cols] = lh
            sk = sink_ref[j, rows, half * hd:half * hd + 1]
            m = jnp.maximum(jnp.max(lh, axis=-1, keepdims=True), sk)
            m_ref[rows, half * LANES:(half + 1) * LANES] = jnp.broadcast_to(m, (rc, LANES))
        for rows, half in units:
            cols = slice(half * span, (half + 1) * span)
            mb = m_ref[rows, half * LANES:(half + 1) * LANES]
            p = jnp.exp(l_ref[rows, cols] - jnp.concatenate([mb] * (span // LANES), axis=1))
            p_ref[rows, cols] = p.astype(BF16)
        probs = p_ref[...]
        pv = jnp.dot(probs, v2, preferred_element_type=F32)
        psum = jnp.dot(probs, ones_ref[...], preferred_element_type=F32)
        m_lane = jnp.where(lane < hd, m_ref[:, :LANES], m_ref[:, LANES:])
        out = pv / (psum + jnp.exp(sink_ref[j] - m_lane))
        for p in range(pairs):
            o_ref[:, base + p * 2 * hd: base + (p + 1) * 2 * hd] = out[p * blk:(p + 1) * blk].astype(o_ref.dtype)


def _attention(cfg, qkv, bias_tbl, sink):
    blk, hd = cfg.window, cfg.head_dim
    aw, kvw = cfg.attn_width, cfg.kv_width
    pairs = cfg.n_heads // cfg.n_kv // 2
    nb_p, nb_s = cfg.seq // blk, cfg.dec_seq // blk
    blocks_p = cfg.batch * nb_p
    n_blocks = cfg.tokens // blk
    _, rel = _t5_bucket_table(cfg)
    rel_tbl = np.tile(rel, (pairs, 2)).astype(np.int32)
    kofs_tbl = np.tile((np.arange(3 * blk) - cfg.window)[None, :], (pairs * blk, 2)).astype(np.int32)
    sink_tbl = jnp.broadcast_to(sink.astype(F32).reshape(cfg.n_kv, pairs, 1, 2, 1),
                                (cfg.n_kv, pairs, blk, 2, hd)).reshape(cfg.n_kv, pairs * blk, 2 * hd)
    ones_tbl = jnp.asarray(np.kron(np.eye(2), np.ones((3 * blk, hd))), BF16)

    def seq_pos(i):
        in_p = i < blocks_p
        return jnp.where(in_p, i % nb_p, (i - blocks_p) % nb_s), jnp.where(in_p, nb_p, nb_s)

    def prev_blk(i):
        pos, _ = seq_pos(i)
        return jnp.where(pos == 0, i, i - 1)

    def next_blk(i):
        pos, nb = seq_pos(i)
        return jnp.where(pos == nb - 1, i, i + 1)

    kcol, vcol = aw // kvw, aw // kvw + 1
    full = lambda shape: pl.BlockSpec(shape, lambda i: (0,) * len(shape))
    return pl.pallas_call(
        functools.partial(_attn_kernel, cfg=cfg, blocks_p=blocks_p), grid=(n_blocks,),
        in_specs=[pl.BlockSpec((blk, aw), lambda i: (i, 0)),
                  pl.BlockSpec((blk, kvw), lambda i: (prev_blk(i), kcol)),
                  pl.BlockSpec((blk, kvw), lambda i: (i, kcol)),
                  pl.BlockSpec((blk, kvw), lambda i: (next_blk(i), kcol)),
                  pl.BlockSpec((blk, kvw), lambda i: (prev_blk(i), vcol)),
                  pl.BlockSpec((blk, kvw), lambda i: (i, vcol)),
                  pl.BlockSpec((blk, kvw), lambda i: (next_blk(i), vcol)),
                  full(bias_tbl.shape), full(sink_tbl.shape), full(rel_tbl.shape), full(kofs_tbl.shape),
                  full(ones_tbl.shape)],
        out_specs=pl.BlockSpec((blk, aw), lambda i: (i, 0)),
        out_shape=jax.ShapeDtypeStruct((cfg.tokens, aw), BF16),
        scratch_shapes=[pltpu.VMEM(rel_tbl.shape, F32), pltpu.VMEM(rel_tbl.shape, F32), pltpu.VMEM(rel_tbl.shape, F32),
                        pltpu.VMEM(rel_tbl.shape, BF16), pltpu.VMEM((pairs * blk, 2 * LANES), F32)],
        compiler_params=_params(40), name="attn")(
            qkv, qkv, qkv, qkv, qkv, qkv, qkv, bias_tbl, sink_tbl, jnp.asarray(rel_tbl), jnp.asarray(kofs_tbl),
            ones_tbl)


def _dft_cos_sin(n):
    ang = 2.0 * np.pi * np.outer(np.arange(n), np.arange(n)) / n
    return np.cos(ang), np.sin(ang)


def _four1_kernel(u_ref, m1_ref, twr_ref, twi_ref, zr_ref, zi_ref, *, n1, t2, width):
    y = jnp.dot(m1_ref[...], u_ref[...], preferred_element_type=F32)
    twr, twi = twr_ref[0], twi_ref[0]
    for l in range(t2):
        sl = slice(l * width, (l + 1) * width)
        yr, yi = y[:n1, sl], y[n1:, sl]
        cr, ci = twr[:, l:l + 1], twi[:, l:l + 1]
        zr_ref[:, sl] = (yr * cr - yi * ci).astype(zr_ref.dtype)
        zi_ref[:, sl] = (yr * ci + yi * cr).astype(zi_ref.dtype)


def _four3_kernel(zr_ref, zi_ref, m3_ref, mc_ref, o_ref, *, n2, n_groups, group, norm):
    z = jnp.concatenate([zr_ref[0], zi_ref[0]], axis=0)
    x = jnp.dot(m3_ref[...], z, preferred_element_type=F32)
    xr, xi = x[:n2].astype(BF16), x[n2:].astype(BF16)
    for g in range(n_groups):
        sl = slice(g * group, (g + 1) * group)
        xg = jnp.concatenate([xr[:, sl], xi[:, sl]], axis=1)
        o_ref[:, sl] = (jnp.dot(xg, mc_ref[...], preferred_element_type=F32) * norm).astype(o_ref.dtype)


def _fourier_group(cfg, u2, row_off, nbatch, seq):
    n2, t2, width = cfg.four_n2, cfg.four_t2, cfg.fourier_width
    n1 = seq // n2
    c1, s1 = _dft_cos_sin(n1)
    m1 = jnp.asarray(np.concatenate([c1, -s1], axis=0), BF16)
    ang = 2.0 * np.pi * np.outer(np.arange(n1), np.arange(n2)) / seq
    twr = jnp.asarray(np.cos(ang).reshape(n1, n2 // t2, t2).transpose(1, 0, 2), F32)
    twi = jnp.asarray((-np.sin(ang)).reshape(n1, n2 // t2, t2).transpose(1, 0, 2), F32)
    blk_off = row_off // n1
    zshape = jax.ShapeDtypeStruct((nbatch * n1, n2 * width), BF16)
    tw_spec = pl.BlockSpec((1, n1, t2), lambda b, j: (j, 0, 0))
    zr, zi = pl.pallas_call(
        functools.partial(_four1_kernel, n1=n1, t2=t2, width=width), grid=(nbatch, n2 // t2),
        in_specs=[pl.BlockSpec((n1, t2 * width), lambda b, j: (blk_off + b, j)),
                  pl.BlockSpec((2 * n1, n1), lambda b, j: (0, 0)), tw_spec, tw_spec],
        out_specs=[pl.BlockSpec((n1, t2 * width), lambda b, j: (b, j))] * 2,
        out_shape=[zshape, zshape],
        compiler_params=_params(32), name="four1")(u2, m1, twr, twi)

    c2, s2 = _dft_cos_sin(n2)
    m3 = jnp.asarray(np.block([[c2, s2], [-s2, c2]]), BF16)
    cc, sc = _dft_cos_sin(cfg.fgroup)
    mc = jnp.asarray(np.concatenate([cc, sc], axis=0), BF16)
    norm = 1.0 / math.sqrt(seq * cfg.fgroup)
    z_spec = pl.BlockSpec((1, n2, width), lambda b, k1: (b * n1 + k1, 0, 0))
    f = pl.pallas_call(
        functools.partial(_four3_kernel, n2=n2, n_groups=cfg.n_fgroups, group=cfg.fgroup, norm=norm),
        grid=(nbatch, n1),
        in_specs=[z_spec, z_spec, pl.BlockSpec((2 * n2, 2 * n2), lambda b, k1: (0, 0)),
                  pl.BlockSpec((2 * cfg.fgroup, cfg.fgroup), lambda b, k1: (0, 0))],
        out_specs=pl.BlockSpec((n2, width), lambda b, k1: (b, k1)),
        out_shape=jax.ShapeDtypeStruct((nbatch * n2, n1 * width), BF16),
        compiler_params=_params(32), name="four3")(
            zr.reshape(nbatch * n1, n2, width), zi.reshape(nbatch * n1, n2, width), m3, mc)
    return f.reshape(nbatch * seq, width)


def _proj_kernel(a_ref, fp_ref, fs_ref, g_ref, wa_ref, wf_ref, bf_ref, o_ref, *, n_p, d):
    i = pl.program_id(0)
    a = jnp.dot(a_ref[...], wa_ref[...], preferred_element_type=F32)

    def run(f_ref):
        f = jnp.dot(f_ref[...], wf_ref[...], preferred_element_type=F32) + bf_ref[...]
        ga = g_ref[:, :d].astype(F32)
        gf = g_ref[:, d:].astype(F32)
        o_ref[...] = (jax.nn.sigmoid(ga) * a + jax.nn.sigmoid(gf) * f).astype(o_ref.dtype)

    pl.when(i < n_p)(lambda: run(fp_ref))
    pl.when(i >= n_p)(lambda: run(fs_ref))


def _proj(cfg, attn, f_p, f_s, gates, wa, wf, bf):
    tm, d, aw, fw = cfg.tm_proj, cfg.d_model, cfg.attn_width, cfg.fourier_width
    const = lambda shape: pl.BlockSpec(shape, lambda i: (0, 0))
    vmem = (2 * (aw + fw) * d * 2 + 2 * tm * (aw + 2 * fw + 3 * d) * 2 + 3 * tm * d * 4) / MIB + 8
    return pl.pallas_call(
        functools.partial(_proj_kernel, n_p=cfg.tokens_p // tm, d=d), grid=(cfg.tokens // tm,),
        in_specs=[pl.BlockSpec((tm, aw), lambda i: (i, 0))] + _two_group_specs(cfg, tm, fw) + [
            pl.BlockSpec((tm, 2 * d), lambda i: (i, 0)), const((aw, d)), const((fw, d)), const((1, d))],
        out_specs=pl.BlockSpec((tm, d), lambda i: (i, 0)),
        out_shape=jax.ShapeDtypeStruct((cfg.tokens, d), BF16),
        compiler_params=_params(vmem), name="proj")(attn, f_p, f_s, gates, wa, wf, bf)


def _resid_kernel(m_ref, wo_ref, xp_ref, xs_ref, gate_ref, o_ref, *, n_p):
    i = pl.program_id(0)
    out = jnp.dot(m_ref[...], wo_ref[...], preferred_element_type=F32)

    def run(x_ref):
        o_ref[...] = x_ref[...] + gate_ref[...] * out

    pl.when(i < n_p)(lambda: run(xp_ref))
    pl.when(i >= n_p)(lambda: run(xs_ref))


def _resid(cfg, mixed, wo, x_p, x_s, ada3):
    tm, d = cfg.tm_proj, cfg.d_model
    vmem = (2 * d * d * 2 + 2 * tm * d * 2 + 7 * tm * d * 4) / MIB + 8
    return pl.pallas_call(
        functools.partial(_resid_kernel, n_p=cfg.tokens_p // tm), grid=(cfg.tokens // tm,),
        in_specs=[pl.BlockSpec((tm, d), lambda i: (i, 0)), pl.BlockSpec((d, d), lambda i: (0, 0))]
        + _two_group_specs(cfg, tm, d) + [_ada_spec(cfg, tm, 2)],
        out_specs=pl.BlockSpec((tm, d), lambda i: (i, 0)),
        out_shape=jax.ShapeDtypeStruct((cfg.tokens, d), F32),
        compiler_params=_params(vmem), name="resid")(mixed, wo, x_p, x_s, ada3)


def _router_kernel(x_ref, g_ref, sh_ref, sc_ref, wrt_ref, rb_ref, tri_ref,
                   hp_ref, idx_ref, wgt_ref, pos_ref, cnt_ref, carry_ref, *, cfg):
    d, ne, ng = cfg.d_model, cfg.n_experts, cfg.n_egroups
    per = ne // ng
    i = pl.program_id(0)

    @pl.when(i == 0)
    def _():
        carry_ref[...] = jnp.zeros_like(carry_ref)

    h = _modulated_norm(x_ref[...], g_ref[...], sh_ref[...], sc_ref[...])
    tm = h.shape[0]
    _store_rows(hp_ref, 0, _pack_bf16_pair(h[:, :d // 2], h[:, d // 2:]))

    logits = lax.dot_general(wrt_ref[...], h, (((1,), (1,)), ((), ())), preferred_element_type=F32,
                             precision=lax.Precision.HIGHEST)
    scores = jax.nn.sigmoid(logits)
    sel = scores + rb_ref[...]
    s3 = sel.reshape(ng, per, tm)
    io3 = lax.broadcasted_iota(I32, (ng, per, tm), 1)
    m1 = jnp.max(s3, axis=1, keepdims=True)
    i1 = jnp.min(jnp.where(s3 == m1, io3, per), axis=1, keepdims=True)
    m2 = jnp.max(jnp.where(io3 == i1, -jnp.inf, s3), axis=1, keepdims=True)
    gs = (m1 + m2).reshape(ng, tm)
    gio = lax.broadcasted_iota(I32, (ng, tm), 0)
    before = jnp.zeros((ng, tm), I32)
    for g2 in range(ng):
        row = gs[g2:g2 + 1, :]
        ahead = jnp.where(row > gs, 1, jnp.where(row == gs, jnp.where(gio > g2, 1, 0), 0))
        before = before + ahead
    keep = jnp.where(before < cfg.topk_groups, 1.0, 0.0)
    keep_e = jnp.broadcast_to(keep.reshape(ng, 1, tm), (ng, per, tm)).reshape(ne, tm)
    cur = jnp.where(keep_e > 0.5, sel, -jnp.inf)

    eio = lax.broadcasted_iota(I32, (ne, tm), 0)
    chosen, picked = [], jnp.zeros((ne, tm), F32)
    for _ in range(cfg.top_k):
        m = jnp.max(cur, axis=0, keepdims=True)
        ij = jnp.min(jnp.where(cur == m, eio, ne), axis=0, keepdims=True)
        hit = eio == ij
        picked = jnp.where(hit, 1.0, picked)
        cur = jnp.where(hit, -jnp.inf, cur)
        chosen.append(ij)
    before_t = jnp.dot(picked.astype(BF16), tri_ref[...], preferred_element_type=F32)
    rank = carry_ref[...] + before_t
    sc_rows, pos_rows = [], []
    for ij in chosen:
        hit = eio == ij
        sc_rows.append(jnp.sum(jnp.where(hit, scores, 0.0), axis=0, keepdims=True))
        pos_rows.append(jnp.sum(jnp.where(hit, rank, 0.0), axis=0, keepdims=True))
    sc_all = jnp.concatenate(sc_rows, axis=0)
    total = jnp.sum(sc_all, axis=0, keepdims=True)
    idx_ref[...] = jnp.concatenate(chosen, axis=0)
    wgt_ref[...] = sc_all / total * cfg.routed_scale
    pos_ref[...] = jnp.concatenate(pos_rows, axis=0).astype(I32)
    carry_ref[...] = carry_ref[...] + jnp.sum(picked, axis=1, keepdims=True)
    cnt_ref[...] = carry_ref[...].astype(I32)


def _router(cfg, x1, g, ada3, w_router, router_bias):
    tm, d, ne, k = cfg.tm_router, cfg.d_model, cfg.n_experts, cfg.top_k
    t = cfg.tokens
    tri = jnp.asarray(np.triu(np.ones((tm, tm), np.float32), 1), BF16)
    row = lambda dt: jax.ShapeDtypeStruct((k, t), dt)
    row_spec = pl.BlockSpec((k, tm), lambda i: (0, i))
    return pl.pallas_call(
        functools.partial(_router_kernel, cfg=cfg), grid=(t // tm,),
        in_specs=[pl.BlockSpec((tm, d), lambda i: (i, 0)), pl.BlockSpec((1, d), lambda i: (0, 0)),
                  _ada_spec(cfg, tm, 3), _ada_spec(cfg, tm, 4),
                  pl.BlockSpec((ne, d), lambda i: (0, 0)), pl.BlockSpec((ne, 1), lambda i: (0, 0)),
                  pl.BlockSpec((tm, tm), lambda i: (0, 0))],
        out_specs=[pl.BlockSpec((tm * _lines(d // 2), LANES), lambda i: (i, 0)), row_spec, row_spec, row_spec,
                   pl.BlockSpec((ne, 1), lambda i: (0, 0))],
        out_shape=[jax.ShapeDtypeStruct((t * _lines(d // 2), LANES), U32), row(I32), row(F32), row(I32),
                   jax.ShapeDtypeStruct((ne, 1), I32)],
        scratch_shapes=[pltpu.VMEM((ne, 1), F32)],
        compiler_params=_params(40), name="router")(
            x1, g, ada3, ada3, w_router.T.astype(F32), router_bias.reshape(ne, 1).astype(F32), tri)


def _slots_kernel(pstart_ref, idx_ref, pos_ref, o_ref, *, n_experts):
    idx = idx_ref[...]
    base = lax.fori_loop(0, n_experts, lambda e, acc: jnp.where(idx == e, pstart_ref[e], acc),
                         jnp.zeros(idx.shape, I32))
    o_ref[...] = base + pos_ref[...]


def _slots(cfg, pstart, idx, pos):
    k, t = idx.shape
    tn = math.gcd(t, 4096)
    spec = pl.BlockSpec((k, tn), lambda i, *_: (0, i))
    grid_spec = pltpu.PrefetchScalarGridSpec(num_scalar_prefetch=1, grid=(t // tn,), in_specs=[spec, spec],
                                             out_specs=spec)
    return pl.pallas_call(functools.partial(_slots_kernel, n_experts=cfg.n_experts), grid_spec=grid_spec,
                          out_shape=jax.ShapeDtypeStruct((k, t), I32), name="slots")(pstart, idx, pos)


def _dispatch_kernel(pend_ref, padded_ref, nt_ref, hp_ref, slot_ref, xs_ref, zero_ref, sem, zsem, *, cfg):
    tm, tme, k, ne = cfg.tm_dispatch, cfg.tm_expert, cfg.top_k, cfg.n_experts
    lines = _lines(cfg.d_model // 2)
    n_tiles = xs_ref.shape[0] // (tme * lines)
    i = pl.program_id(0)

    def pad_copy(e):
        return pltpu.make_async_copy(zero_ref, xs_ref.at[_row_window(pend_ref[e] - tme, tme, lines)], zsem)

    def tail_copy(t):
        return pltpu.make_async_copy(zero_ref, xs_ref.at[_row_window(t * tme, tme, lines)], zsem)

    @pl.when(i == 0)
    def _():
        zero_ref[...] = jnp.zeros_like(zero_ref)

        def start(e, c):
            pl.when(padded_ref[e] > 0)(lambda: pad_copy(e).start())
            return c

        def wait(e, c):
            pl.when(padded_ref[e] > 0)(lambda: pad_copy(e).wait())
            return c

        lax.fori_loop(0, ne, start, 0)
        lax.fori_loop(nt_ref[0], n_tiles, lambda t, c: (tail_copy(t).start(), c)[1], 0)
        lax.fori_loop(0, ne, wait, 0)
        lax.fori_loop(nt_ref[0], n_tiles, lambda t, c: (tail_copy(t).wait(), c)[1], 0)

    def row(r, c):
        for j in range(k):
            pltpu.make_async_copy(hp_ref.at[_row_window(r, 1, lines)],
                                  xs_ref.at[_row_window(slot_ref[j, r], 1, lines)], sem).start(priority=j % 2)
        return c

    lax.fori_loop(0, tm, row, 0)
    for _ in range(k):
        pltpu.make_async_copy(hp_ref, xs_ref.at[_row_window(0, tm, lines)], sem).wait()


def _dispatch(cfg, hp, slot, pend, padded, n_used, n_slots):
    tm, d, k = cfg.tm_dispatch, cfg.d_model, cfg.top_k
    lines = _lines(d // 2)
    grid_spec = pltpu.PrefetchScalarGridSpec(
        num_scalar_prefetch=3, grid=(cfg.tokens // tm,),
        in_specs=[pl.BlockSpec((tm * lines, LANES), lambda i, *_: (i, 0)),
                  pl.BlockSpec((k, tm), lambda i, *_: (0, i), memory_space=pltpu.SMEM)],
        out_specs=pl.BlockSpec(memory_space=pl.ANY),
        scratch_shapes=[pltpu.VMEM((cfg.tm_expert * lines, LANES), U32), pltpu.SemaphoreType.DMA,
                        pltpu.SemaphoreType.DMA])
    return pl.pallas_call(
        functools.partial(_dispatch_kernel, cfg=cfg), grid_spec=grid_spec,
        out_shape=jax.ShapeDtypeStruct((n_slots * lines, LANES), U32),
        compiler_params=_params(24), name="dispatch")(pend, padded, n_used, hp, slot)


def _expert_kernel(te_ref, nt_ref, nxt_ref, xs_ref, wg_hbm, wu_hbm, wd_hbm, ys_ref,
                   wgf, wuf, wdf, wgb, wub, wdb, side_ref, wsem):
    i = pl.program_id(0)
    nt = nt_ref[0]
    tme = xs_ref.shape[0] // _lines(wdb.shape[1] // 2)
    half = wdb.shape[1] // 2

    def weight_copies(e, s):
        return [pltpu.make_async_copy(src.at[e], dst.at[s], wsem.at[s])
                for src, dst in ((wg_hbm, wgf), (wu_hbm, wuf), (wd_hbm, wdf))]

    @pl.when(i >= nt)
    def _():
        ys_ref[...] = jnp.zeros_like(ys_ref)

    @pl.when(i < nt)
    def _():
        e = te_ref[i]

        @pl.when(i == 0)
        def _():
            side_ref[0] = 0
            for c in weight_copies(e, 0):
                c.start()

        @pl.when(jnp.logical_or(i == 0, e != te_ref[jnp.maximum(i - 1, 0)]))
        def _():
            s = side_ref[0]
            nxt = nxt_ref[e]
            @pl.when(nxt >= 0)
            def _():
                for c in weight_copies(nxt, 1 - s):
                    c.start()

            for c in weight_copies(e, s):
                c.wait()
            wgb[...] = wgf[s].astype(BF16)
            wub[...] = wuf[s].astype(BF16)
            wdb[...] = wdf[s].astype(BF16)
            side_ref[0] = 1 - s

        lo, hi = _unpack_bf16_pair(_load_rows(xs_ref, 0, tme, half))
        x = jnp.concatenate([lo.astype(BF16), hi.astype(BF16)], axis=1)
        g = jnp.dot(x, wgb[...], preferred_element_type=F32)
        u = jnp.dot(x, wub[...], preferred_element_type=F32)
        a = (g * jax.nn.sigmoid(g) * u).astype(BF16)
        cb = _pair_block(2 * half)
        lines = _lines(half)
        for m in range(half // cb):
            ym = jnp.dot(a, wdb[:, 2 * m * cb:(2 * m + 2) * cb], preferred_element_type=F32)
            packed = _pack_bf16_pair(ym[:, :cb], ym[:, cb:])
            for c in range(cb // LANES):
                ys_ref[pl.ds(m * (cb // LANES) + c, tme, stride=lines), :] = packed[:, c * LANES:(c + 1) * LANES]


def _experts(cfg, xs, tile_expert, n_used, next_expert, wg, wu, wd):
    tme, d, de = cfg.tm_expert, cfg.d_model, cfg.d_expert
    lines = _lines(d // 2)
    n_tiles = xs.shape[0] // (tme * lines)
    any_spec = pl.BlockSpec(memory_space=pl.ANY)
    grid_spec = pltpu.PrefetchScalarGridSpec(
        num_scalar_prefetch=3, grid=(n_tiles,),
        in_specs=[pl.BlockSpec((tme * lines, LANES), lambda i, te, nt, nx: (jnp.minimum(i, nt[0] - 1), 0)),
                  any_spec, any_spec, any_spec],
        out_specs=pl.BlockSpec((tme * lines, LANES), lambda i, te, nt, nx: (i, 0)),
        scratch_shapes=[pltpu.VMEM((2, d, de), F32), pltpu.VMEM((2, d, de), F32), pltpu.VMEM((2, de, d), F32),
                        pltpu.VMEM((d, de), BF16), pltpu.VMEM((d, de), BF16), pltpu.VMEM((de, d), BF16),
                        pltpu.SMEM((1,), I32), pltpu.SemaphoreType.DMA((2,))])
    vmem = (2 * 3 * d * de * 4 + 3 * d * de * 2 + 6 * tme * d * 2 + 6 * tme * d * 4) / MIB + 8
    return pl.pallas_call(
        _expert_kernel, grid_spec=grid_spec, out_shape=jax.ShapeDtypeStruct(xs.shape, U32),
        compiler_params=_params(vmem), name="experts")(tile_expert, n_used, next_expert, xs, wg, wu, wd)


def _final_kernel(slot_ref, slotn_ref, x_ref, wt_ref, g_ref, sh_ref, sc_ref, gate_ref, wg_ref, wu_ref, wd_ref,
                  nf_ref, ys_ref, o_ref, buf0, buf1, sem, *, cfg):
    tm, k = cfg.tm_final, cfg.top_k
    half = cfg.d_model // 2
    lines = _lines(half)
    bufs = (buf0, buf1)
    i = pl.program_id(0)
    n = pl.num_programs(0)

    def row_copy(slots, b, j, r):
        return pltpu.make_async_copy(ys_ref.at[_row_window(slots[j, r], 1, lines)],
                                     bufs[b].at[_row_window(j * tm + r, 1, lines)], sem.at[b])

    def wait_planes(b):
        for j in range(k):
            pltpu.make_async_copy(ys_ref.at[_row_window(0, tm, lines)],
                                  bufs[b].at[_row_window(j * tm, tm, lines)], sem.at[b]).wait()

    @pl.when(i == 0)
    def _():
        def row(r, c):
            for j in range(k):
                row_copy(slot_ref, 0, j, r).start(priority=j % 2)
            return c

        lax.fori_loop(0, tm, row, 0)

    def step(b):
        wait_planes(b)
        for r in range(tm):
            for j in range(k):
                row_copy(slotn_ref, 1 - b, j, r).start(priority=j % 2)

        x = x_ref[...]
        h = _modulated_norm(x, g_ref[...], sh_ref[...], sc_ref[...]).astype(BF16)
        g = jnp.dot(h, wg_ref[...], preferred_element_type=F32)
        u = jnp.dot(h, wu_ref[...], preferred_element_type=F32)
        y = jnp.dot((g * jax.nn.sigmoid(g) * u).astype(BF16), wd_ref[...], preferred_element_type=F32)
        cb = _pair_block(2 * half)
        for j in range(k):
            lo, hi = _unpack_bf16_pair(_load_rows(bufs[b], j * tm, tm, half))
            blocks = [part[:, m * cb:(m + 1) * cb] for m in range(half // cb) for part in (lo, hi)]
            y = y + wt_ref[:, j:j + 1] * jnp.concatenate(blocks, axis=1)
        xo = x + gate_ref[...] * y
        o_ref[...] = xo * lax.rsqrt(jnp.mean(xo * xo, axis=-1, keepdims=True) + EPS) * nf_ref[...]
        pl.when(i == n - 1)(lambda: wait_planes(1 - b))

    for b in range(2):
        pl.when(i % 2 == b)(functools.partial(step, b))


def _final_group(cfg, tile_off, batch_off, nbatch, seq, x1, ys, slot, wgt_t, ada3, g, wgs, wus, wds, nf):
    tm, d, k, ds = cfg.tm_final, cfg.d_model, cfg.top_k, cfg.d_shared
    n = nbatch * seq // tm
    per_batch = seq // tm
    ada = lambda which: pl.BlockSpec((None, 1, d), lambda i: ((batch_off + i // per_batch) * 6 + which, 0, 0))
    const = lambda shape: pl.BlockSpec(shape, lambda i: (0, 0))
    vmem = (2 * k * tm * d * 2 + 2 * 3 * d * ds * 2 + 10 * tm * d * 4) / MIB + 8
    return pl.pallas_call(
        functools.partial(_final_kernel, cfg=cfg), grid=(n,),
        in_specs=[pl.BlockSpec((k, tm), lambda i: (0, tile_off + i), memory_space=pltpu.SMEM),
                  pl.BlockSpec((k, tm), lambda i: (0, tile_off + jnp.minimum(i + 1, n - 1)), memory_space=pltpu.SMEM),
                  pl.BlockSpec((tm, d), lambda i: (tile_off + i, 0)),
                  pl.BlockSpec((tm, k), lambda i: (tile_off + i, 0)),
                  const((1, d)), ada(3), ada(4), ada(5), const((d, ds)), const((d, ds)), const((ds, d)),
                  const((1, d)), pl.BlockSpec(memory_space=pl.ANY)],
        out_specs=pl.BlockSpec((tm, d), lambda i: (i, 0)),
        out_shape=jax.ShapeDtypeStruct((nbatch * seq, d), F32),
        scratch_shapes=[pltpu.VMEM((k * tm * _lines(d // 2), LANES), U32),
                        pltpu.VMEM((k * tm * _lines(d // 2), LANES), U32), pltpu.SemaphoreType.DMA((2,))],
        compiler_params=_params(vmem), name="final")(
            slot, slot, x1, wgt_t, g, ada3, ada3, ada3, wgs, wus, wds, nf, ys)


def _forward(cfg, x_prompt, x_sample, c_prompt, c_sample, w_ada, b_ada, norm_mix, w_in, rel_bias, sink,
             w_attn_proj, w_four, b_four, w_out, norm_ffn, w_router, router_bias, w_gate_e, w_up_e, w_down_e,
             w_gate_s, w_up_s, w_down_s, norm_final):
    d, t = cfg.d_model, cfg.tokens
    aw, kvw, fw = cfg.attn_width, cfg.kv_width, cfg.fourier_width
    x_p = x_prompt.reshape(cfg.tokens_p, d)
    x_s = x_sample.reshape(cfg.tokens_s, d)

    nc = cfg.batch + cfg.dec_batch
    rows = -(-nc // 8) * 8
    c_all = jnp.concatenate([c_prompt, c_sample, jnp.zeros((rows - nc, d), F32)], axis=0)
    ada3 = _ada(cfg, c_all, w_ada[0], b_ada).reshape(rows * 6, 1, d)

    h = _modnorm(cfg, x_p, x_s, norm_mix, ada3)
    w_in_b = w_in[0].astype(BF16)
    c0, c1 = aw + 2 * kvw, aw + 2 * kvw + fw
    qkv = _matmul(h, w_in_b[:, :c0], cfg.tm_mm, c0, "mm_qkv")
    u = _matmul(h, w_in_b[:, c0:c1], cfg.tm_mm, fw, "mm_u")
    gates = _matmul(h, w_in_b[:, c1:], cfg.tm_mm, d, "mm_gates")

    attn = _attention(cfg, qkv, _bias_table(cfg, rel_bias), sink[0])
    u2 = u.reshape(t // cfg.four_n2, cfg.four_n2 * fw)
    f_p = _fourier_group(cfg, u2, 0, cfg.batch, cfg.seq)
    f_s = _fourier_group(cfg, u2, cfg.tokens_p // cfg.four_n2, cfg.dec_batch, cfg.dec_seq)
    mixed = _proj(cfg, attn, f_p, f_s, gates, w_attn_proj[0].astype(BF16), w_four[0].astype(BF16), b_four)
    x1 = _resid(cfg, mixed, w_out[0].astype(BF16), x_p, x_s, ada3)

    hp, idx, wgt, pos, cnt = _router(cfg, x1, norm_ffn, ada3, w_router[0], router_bias[0])
    tme, ne = cfg.tm_expert, cfg.n_experts
    counts = cnt[:, 0]
    padded = (counts + tme - 1) // tme * tme
    pend = jnp.cumsum(padded).astype(I32)
    slot = _slots(cfg, pend - padded, idx, pos)
    n_tiles = t * cfg.top_k // tme + ne
    tile_start = jnp.arange(n_tiles, dtype=I32) * tme
    tile_expert = jnp.minimum(jnp.sum((pend[None, :] <= tile_start[:, None]).astype(I32), axis=1), ne - 1)
    n_used = (pend[-1:] // tme).astype(I32)
    owner = jnp.where(counts > 0, jnp.arange(ne, dtype=I32), ne)
    later = jnp.concatenate([lax.cummin(owner, reverse=True)[1:], jnp.full((1,), ne, I32)])
    next_expert = jnp.where(later < ne, later, -1).astype(I32)
    xs = _dispatch(cfg, hp, slot, pend, padded, n_used, n_tiles * tme)
    ys = _experts(cfg, xs, tile_expert, n_used, next_expert, w_gate_e[0], w_up_e[0], w_down_e[0])

    wgt_t = wgt.T
    shared = (w_gate_s[0].astype(BF16), w_up_s[0].astype(BF16), w_down_s[0].astype(BF16))
    nf = norm_final.reshape(1, d)
    y_p = _final_group(cfg, 0, 0, cfg.batch, cfg.seq, x1, ys, slot, wgt_t, ada3, norm_ffn, *shared, nf)
    y_s = _final_group(cfg, cfg.tokens_p // cfg.tm_final, cfg.batch, cfg.dec_batch, cfg.dec_seq, x1, ys, slot,
                       wgt_t, ada3, norm_ffn, *shared, nf)
    return (y_p.reshape(cfg.batch, cfg.seq, d), y_s.reshape(cfg.dec_batch, cfg.dec_seq, d))


def kernel(x_prompt, x_sample, c_prompt, c_sample, w_ada, b_ada, norm_mix, w_in, rel_bias, sink, w_attn_proj,
           w_four, b_four, w_out, norm_ffn, w_router, router_bias, w_gate_e, w_up_e, w_down_e, w_gate_s, w_up_s,
           w_down_s, norm_final):
    return _forward(Cfg(), x_prompt, x_sample, c_prompt, c_sample, w_ada, b_ada, norm_mix, w_in, rel_bias, sink,
                    w_attn_proj, w_four, b_four, w_out, norm_ffn, w_router, router_bias, w_gate_e, w_up_e,
                    w_down_e, w_gate_s, w_up_s, w_down_s, norm_final)
```

```python
import functools
import math
from typing import NamedTuple

import jax
import jax.numpy as jnp
import numpy as np
from jax import lax
from jax.experimental import pallas as pl
from jax.experimental.pallas import tpu as pltpu

F32 = jnp.float32
BF16 = jnp.bfloat16
I32 = jnp.int32
U32 = jnp.uint32

EPS = 1e-6
NEG_INF = -1e30
LANES = 128
MIB = 1024 * 1024


class Cfg(NamedTuple):
    d_model: int = 2048
    batch: int = 4
    seq: int = 8192
    dec_batch: int = 8
    dec_seq: int = 2048
    n_heads: int = 32
    n_kv: int = 4
    head_dim: int = 64
    window: int = 128
    n_buckets: int = 32
    max_distance: int = 128
    n_fgroups: int = 8
    fgroup: int = 128
    n_experts: int = 256
    top_k: int = 8
    n_egroups: int = 8
    topk_groups: int = 4
    d_expert: int = 512
    d_shared: int = 512
    routed_scale: float = 2.5
    tm_norm: int = 512
    tm_mm: int = 512
    tm_proj: int = 256
    tm_router: int = 256
    tm_dispatch: int = 256
    tm_expert: int = 256
    tm_final: int = 256
    four_n2: int = 128
    four_t2: int = 8

    @property
    def attn_width(self):
        return self.n_heads * self.head_dim

    @property
    def kv_width(self):
        return self.n_kv * self.head_dim

    @property
    def fourier_width(self):
        return self.n_fgroups * self.fgroup

    @property
    def tokens_p(self):
        return self.batch * self.seq

    @property
    def tokens_s(self):
        return self.dec_batch * self.dec_seq

    @property
    def tokens(self):
        return self.tokens_p + self.tokens_s


def _params(vmem_mib):
    return pltpu.CompilerParams(vmem_limit_bytes=int(vmem_mib * MIB))


def _batch_of_tile(i, cfg, tm):
    n_p = cfg.tokens_p // tm
    return jnp.where(i < n_p, i // (cfg.seq // tm), cfg.batch + (i - n_p) // (cfg.dec_seq // tm))


def _ada_spec(cfg, tm, which):
    return pl.BlockSpec((None, 1, cfg.d_model), lambda i: (_batch_of_tile(i, cfg, tm) * 6 + which, 0, 0))


def _two_group_specs(cfg, tm, width):
    n_p = cfg.tokens_p // tm
    return [pl.BlockSpec((tm, width), lambda i: (jnp.minimum(i, n_p - 1), 0)),
            pl.BlockSpec((tm, width), lambda i: (jnp.maximum(i - n_p, 0), 0))]


def _modulated_norm(x, g, shift, scale):
    y = x * lax.rsqrt(jnp.mean(x * x, axis=-1, keepdims=True) + EPS) * g
    return y * (1.0 + scale) + shift


def _pack_bf16_pair(lo, hi):
    lo_bits = pltpu.bitcast(lo.astype(BF16).astype(F32), U32) >> 16
    hi_bits = pltpu.bitcast(hi.astype(BF16).astype(F32), U32) & jnp.uint32(0xFFFF0000)
    return hi_bits | lo_bits


def _unpack_bf16_pair(p):
    lo = pltpu.bitcast(p << 16, F32)
    hi = pltpu.bitcast(p & jnp.uint32(0xFFFF0000), F32)
    return lo, hi


def _pair_block(d):
    return min(256, d // 2)


def _lines(width):
    return width // LANES


def _row_window(row, n_rows, lines):
    return pl.ds(pl.multiple_of(row * lines, lines), n_rows * lines)


def _load_rows(ref, row, n_rows, width):
    lines = _lines(width)
    base = pl.multiple_of(row * lines, lines) if not isinstance(row, int) else row * lines
    return jnp.concatenate([ref[pl.ds(base + c, n_rows, stride=lines), :] for c in range(lines)], axis=1)


def _store_rows(ref, row, mat):
    n_rows, width = mat.shape
    lines = _lines(width)
    for c in range(lines):
        ref[pl.ds(row * lines + c, n_rows, stride=lines), :] = mat[:, c * LANES:(c + 1) * LANES]


def _ada_kernel(c_ref, w_ref, b_ref, o_ref):
    c = c_ref[...]
    s = c * jax.nn.sigmoid(c)
    o_ref[...] = jnp.dot(s, w_ref[...], preferred_element_type=F32, precision=lax.Precision.HIGHEST) + b_ref[...]


def _ada(cfg, c_all, w_ada, b_ada):
    rows, d = c_all.shape
    n = w_ada.shape[1]
    tn = n // 8
    return pl.pallas_call(
        _ada_kernel, grid=(n // tn,),
        in_specs=[pl.BlockSpec((rows, d), lambda j: (0, 0)), pl.BlockSpec((d, tn), lambda j: (0, j)),
                  pl.BlockSpec((1, tn), lambda j: (0, j))],
        out_specs=pl.BlockSpec((rows, tn), lambda j: (0, j)),
        out_shape=jax.ShapeDtypeStruct((rows, n), F32),
        compiler_params=_params(2 * d * tn * 4 / MIB + 8), name="ada")(c_all, w_ada, b_ada)


def _modnorm_kernel(xp_ref, xs_ref, g_ref, sh_ref, sc_ref, o_ref, *, n_p):
    i = pl.program_id(0)

    def run(x_ref):
        o_ref[...] = _modulated_norm(x_ref[...], g_ref[...], sh_ref[...], sc_ref[...]).astype(o_ref.dtype)

    pl.when(i < n_p)(lambda: run(xp_ref))
    pl.when(i >= n_p)(lambda: run(xs_ref))


def _modnorm(cfg, x_p, x_s, g, ada3):
    tm, d = cfg.tm_norm, cfg.d_model
    return pl.pallas_call(
        functools.partial(_modnorm_kernel, n_p=cfg.tokens_p // tm), grid=(cfg.tokens // tm,),
        in_specs=_two_group_specs(cfg, tm, d) + [pl.BlockSpec((1, d), lambda i: (0, 0)),
                                                 _ada_spec(cfg, tm, 0), _ada_spec(cfg, tm, 1)],
        out_specs=pl.BlockSpec((tm, d), lambda i: (i, 0)),
        out_shape=jax.ShapeDtypeStruct((cfg.tokens, d), BF16),
        compiler_params=_params(7 * tm * d * 4 / MIB + 8), name="modnorm")(x_p, x_s, g, ada3, ada3)


def _mm_kernel(a_ref, w_ref, o_ref):
    o_ref[...] = jnp.dot(a_ref[...], w_ref[...], preferred_element_type=F32).astype(o_ref.dtype)


def _matmul(a, w, tm, tn, name):
    m, k = a.shape
    n = w.shape[1]
    vmem = (2 * tm * k * 2 + 2 * k * tn * 2 + 2 * tm * tn * 2 + tm * tn * 4) / MIB + 8
    return pl.pallas_call(
        _mm_kernel, grid=(n // tn, m // tm),
        in_specs=[pl.BlockSpec((tm, k), lambda j, i: (i, 0)), pl.BlockSpec((k, tn), lambda j, i: (0, j))],
        out_specs=pl.BlockSpec((tm, tn), lambda j, i: (i, j)),
        out_shape=jax.ShapeDtypeStruct((m, n), BF16),
        compiler_params=_params(vmem), name=name)(a, w)


def _t5_bucket_table(cfg):
    blk, span = cfg.window, 3 * cfg.window
    rel = (np.arange(span)[None, :] - cfg.window - np.arange(blk)[:, None]).astype(np.int32)
    nb = cfg.n_buckets // 2
    max_exact = nb // 2
    ret = np.where(rel > 0, nb, 0)
    n = np.abs(rel)
    nf = np.maximum(n, 1).astype(np.float32)
    ratio = np.log(nf / np.float32(max_exact)) / np.float32(math.log(cfg.max_distance / max_exact))
    large = max_exact + (ratio * np.float32(nb - max_exact)).astype(np.int32)
    large = np.minimum(large, nb - 1)
    return (ret + np.where(n < max_exact, n, large)).astype(np.int32), rel


def _bias_kernel(bucket_ref, rbt_ref, o_ref, *, n_buckets):
    bucket = bucket_ref[...]
    acc = jnp.zeros(o_ref.shape, F32)
    for b in range(n_buckets):
        acc = jnp.where(bucket == b, rbt_ref[:, b:b + 1], acc)
    o_ref[...] = acc


def _bias_table(cfg, rel_bias):
    buckets, _ = _t5_bucket_table(cfg)
    blk, span, h = cfg.window, 3 * cfg.window, cfg.n_heads
    n = blk * span
    flat = pl.pallas_call(
        functools.partial(_bias_kernel, n_buckets=cfg.n_buckets),
        out_shape=jax.ShapeDtypeStruct((h, n), F32), name="bias")(
            jnp.asarray(buckets.reshape(1, n)), rel_bias.T.astype(F32))
    pairs = h // cfg.n_kv // 2
    t = flat.reshape(cfg.n_kv, pairs, 2, blk, span)
    return jnp.transpose(t, (0, 1, 3, 2, 4)).reshape(cfg.n_kv, pairs * blk, 2 * span)


def _attn_kernel(q_ref, kp_ref, kc_ref, kn_ref, vp_ref, vc_ref, vn_ref, bias_ref, sink_ref, rel_ref, kofs_ref,
                 ones_ref, o_ref, valid_ref, s_ref, l_ref, p_ref, m_ref, *, cfg, blocks_p):
    blk, hd = cfg.window, cfg.head_dim
    span = 3 * blk
    pairs = cfg.n_heads // cfg.n_kv // 2
    rc = 64
    nb_p, nb_s = cfg.seq // blk, cfg.dec_seq // blk
    i = pl.program_id(0)
    in_p = i < blocks_p
    pos = jnp.where(in_p, i % nb_p, (i - blocks_p) % nb_s)
    seq_len = jnp.where(in_p, cfg.seq, cfg.dec_seq)
    at_edge = jnp.logical_or(pos == 0, (pos + 1) * blk == seq_len)

    @pl.when(at_edge)
    def _():
        kpos = pos * blk + kofs_ref[...]
        in_band = jnp.abs(rel_ref[...]) <= cfg.window
        in_seq = (kpos >= 0) & (kpos < seq_len)
        valid_ref[...] = jnp.where(in_band, jnp.where(in_seq, 1.0, 0.0), 0.0)

    @pl.when(jnp.logical_not(at_edge))
    def _():
        valid_ref[...] = jnp.where(jnp.abs(rel_ref[...]) <= cfg.window, 1.0, 0.0)

    k = jnp.concatenate([kp_ref[...], kc_ref[...], kn_ref[...]], axis=0)
    v = jnp.concatenate([vp_ref[...], vc_ref[...], vn_ref[...]], axis=0)
    scale = hd ** -0.5
    lane = lax.broadcasted_iota(I32, (pairs * blk, 2 * hd), 1)
    for j in range(cfg.n_kv):
        kj = k[:, j * hd:(j + 1) * hd]
        vj = v[:, j * hd:(j + 1) * hd]
        zero = jnp.zeros_like(kj)
        k2 = jnp.concatenate([jnp.concatenate([kj, zero], axis=1), jnp.concatenate([zero, kj], axis=1)], axis=0)
        v2 = jnp.concatenate([jnp.concatenate([vj, zero], axis=1), jnp.concatenate([zero, vj], axis=1)], axis=0)
        base = j * pairs * 2 * hd
        qs = jnp.concatenate([q_ref[:, base + p * 2 * hd: base + (p + 1) * 2 * hd] for p in range(pairs)], axis=0)
        s_ref[...] = lax.dot_general(qs, k2, (((1,), (1,)), ((), ())), preferred_element_type=F32)
        units = [(slice(r0, r0 + rc), half) for r0 in range(0, pairs * blk, rc) for half in range(2)]

        for rows, half in units:
            cols = slice(half * span, (half + 1) * span)
            lh = s_ref[rows, cols] * scale + bias_ref[j, rows, cols]
            lh = jnp.where(valid_ref[rows, cols] > 0.5, lh, NEG_INF)
            l_ref[rows, cols] = lh
            sk = sink_ref[j, rows, half * hd:half * hd + 1]
            m = jnp.maximum(jnp.max(lh, axis=-1, keepdims=True), sk)
            m_ref[rows, half * LANES:(half + 1) * LANES] = jnp.broadcast_to(m, (rc, LANES))
        for rows, half in units:
            cols = slice(half * span, (half + 1) * span)
            mb = m_ref[rows, half * LANES:(half + 1) * LANES]
            p = jnp.exp(l_ref[rows, cols] - jnp.concatenate([mb] * (span // LANES), axis=1))
            p_ref[rows, cols] = p.astype(BF16)
        probs = p_ref[...]
        pv = jnp.dot(probs, v2, preferred_element_type=F32)
        psum = jnp.dot(probs, ones_ref[...], preferred_element_type=F32)
        m_lane = jnp.where(lane < hd, m_ref[:, :LANES], m_ref[:, LANES:])
        out = pv / (psum + jnp.exp(sink_ref[j] - m_lane))
        for p in range(pairs):
            o_ref[:, base + p * 2 * hd: base + (p + 1) * 2 * hd] = out[p * blk:(p + 1) * blk].astype(o_ref.dtype)


def _attention(cfg, qkv, bias_tbl, sink):
    blk, hd = cfg.window, cfg.head_dim
    aw, kvw = cfg.attn_width, cfg.kv_width
    pairs = cfg.n_heads // cfg.n_kv // 2
    nb_p, nb_s = cfg.seq // blk, cfg.dec_seq // blk
    blocks_p = cfg.batch * nb_p
    n_blocks = cfg.tokens // blk
    _, rel = _t5_bucket_table(cfg)
    rel_tbl = np.tile(rel, (pairs, 2)).astype(np.int32)
    kofs_tbl = np.tile((np.arange(3 * blk) - cfg.window)[None, :], (pairs * blk, 2)).astype(np.int32)
    sink_tbl = jnp.broadcast_to(sink.astype(F32).reshape(cfg.n_kv, pairs, 1, 2, 1),
                                (cfg.n_kv, pairs, blk, 2, hd)).reshape(cfg.n_kv, pairs * blk, 2 * hd)
    ones_tbl = jnp.asarray(np.kron(np.eye(2), np.ones((3 * blk, hd))), BF16)

    def seq_pos(i):
        in_p = i < blocks_p
        return jnp.where(in_p, i % nb_p, (i - blocks_p) % nb_s), jnp.where(in_p, nb_p, nb_s)

    def prev_blk(i):
        pos, _ = seq_pos(i)
        return jnp.where(pos == 0, i, i - 1)

    def next_blk(i):
        pos, nb = seq_pos(i)
        return jnp.where(pos == nb - 1, i, i + 1)

    kcol, vcol = aw // kvw, aw // kvw + 1
    full = lambda shape: pl.BlockSpec(shape, lambda i: (0,) * len(shape))
    return pl.pallas_call(
        functools.partial(_attn_kernel, cfg=cfg, blocks_p=blocks_p), grid=(n_blocks,),
        in_specs=[pl.BlockSpec((blk, aw), lambda i: (i, 0)),
                  pl.BlockSpec((blk, kvw), lambda i: (prev_blk(i), kcol)),
                  pl.BlockSpec((blk, kvw), lambda i: (i, kcol)),
                  pl.BlockSpec((blk, kvw), lambda i: (next_blk(i), kcol)),
                  pl.BlockSpec((blk, kvw), lambda i: (prev_blk(i), vcol)),
                  pl.BlockSpec((blk, kvw), lambda i: (i, vcol)),
                  pl.BlockSpec((blk, kvw), lambda i: (next_blk(i), vcol)),
                  full(bias_tbl.shape), full(sink_tbl.shape), full(rel_tbl.shape), full(kofs_tbl.shape),
                  full(ones_tbl.shape)],
        out_specs=pl.BlockSpec((blk, aw), lambda i: (i, 0)),
        out_shape=jax.ShapeDtypeStruct((cfg.tokens, aw), BF16),
        scratch_shapes=[pltpu.VMEM(rel_tbl.shape, F32), pltpu.VMEM(rel_tbl.shape, F32), pltpu.VMEM(rel_tbl.shape, F32),
                        pltpu.VMEM(rel_tbl.shape, BF16), pltpu.VMEM((pairs * blk, 2 * LANES), F32)],
        compiler_params=_params(40), name="attn")(
            qkv, qkv, qkv, qkv, qkv, qkv, qkv, bias_tbl, sink_tbl, jnp.asarray(rel_tbl), jnp.asarray(kofs_tbl),
            ones_tbl)


def _dft_cos_sin(n):
    ang = 2.0 * np.pi * np.outer(np.arange(n), np.arange(n)) / n
    return np.cos(ang), np.sin(ang)


def _four1_kernel(u_ref, m1_ref, twr_ref, twi_ref, zr_ref, zi_ref, *, n1, t2, width):
    y = jnp.dot(m1_ref[...], u_ref[...], preferred_element_type=F32)
    twr, twi = twr_ref[0], twi_ref[0]
    for l in range(t2):
        sl = slice(l * width, (l + 1) * width)
        yr, yi = y[:n1, sl], y[n1:, sl]
        cr, ci = twr[:, l:l + 1], twi[:, l:l + 1]
        zr_ref[:, sl] = (yr * cr - yi * ci).astype(zr_ref.dtype)
        zi_ref[:, sl] = (yr * ci + yi * cr).astype(zi_ref.dtype)


def _four3_kernel(zr_ref, zi_ref, m3_ref, mc_ref, o_ref, *, n2, n_groups, group, norm):
    z = jnp.concatenate([zr_ref[0], zi_ref[0]], axis=0)
    x = jnp.dot(m3_ref[...], z, preferred_element_type=F32)
    xr, xi = x[:n2].astype(BF16), x[n2:].astype(BF16)
    for g in range(n_groups):
        sl = slice(g * group, (g + 1) * group)
        xg = jnp.concatenate([xr[:, sl], xi[:, sl]], axis=1)
        o_ref[:, sl] = (jnp.dot(xg, mc_ref[...], preferred_element_type=F32) * norm).astype(o_ref.dtype)


def _fourier_group(cfg, u2, row_off, nbatch, seq):
    n2, t2, width = cfg.four_n2, cfg.four_t2, cfg.fourier_width
    n1 = seq // n2
    c1, s1 = _dft_cos_sin(n1)
    m1 = jnp.asarray(np.concatenate([c1, -s1], axis=0), BF16)
    ang = 2.0 * np.pi * np.outer(np.arange(n1), np.arange(n2)) / seq
    twr = jnp.asarray(np.cos(ang).reshape(n1, n2 // t2, t2).transpose(1, 0, 2), F32)
    twi = jnp.asarray((-np.sin(ang)).reshape(n1, n2 // t2, t2).transpose(1, 0, 2), F32)
    blk_off = row_off // n1
    zshape = jax.ShapeDtypeStruct((nbatch * n1, n2 * width), BF16)
    tw_spec = pl.BlockSpec((1, n1, t2), lambda b, j: (j, 0, 0))
    zr, zi = pl.pallas_call(
        functools.partial(_four1_kernel, n1=n1, t2=t2, width=width), grid=(nbatch, n2 // t2),
        in_specs=[pl.BlockSpec((n1, t2 * width), lambda b, j: (blk_off + b, j)),
                  pl.BlockSpec((2 * n1, n1), lambda b, j: (0, 0)), tw_spec, tw_spec],
        out_specs=[pl.BlockSpec((n1, t2 * width), lambda b, j: (b, j))] * 2,
        out_shape=[zshape, zshape],
        compiler_params=_params(32), name="four1")(u2, m1, twr, twi)

    c2, s2 = _dft_cos_sin(n2)
    m3 = jnp.asarray(np.block([[c2, s2], [-s2, c2]]), BF16)
    cc, sc = _dft_cos_sin(cfg.fgroup)
    mc = jnp.asarray(np.concatenate([cc, sc], axis=0), BF16)
    norm = 1.0 / math.sqrt(seq * cfg.fgroup)
    z_spec = pl.BlockSpec((1, n2, width), lambda b, k1: (b * n1 + k1, 0, 0))
    f = pl.pallas_call(
        functools.partial(_four3_kernel, n2=n2, n_groups=cfg.n_fgroups, group=cfg.fgroup, norm=norm),
        grid=(nbatch, n1),
        in_specs=[z_spec, z_spec, pl.BlockSpec((2 * n2, 2 * n2), lambda b, k1: (0, 0)),
                  pl.BlockSpec((2 * cfg.fgroup, cfg.fgroup), lambda b, k1: (0, 0))],
        out_specs=pl.BlockSpec((n2, width), lambda b, k1: (b, k1)),
        out_shape=jax.ShapeDtypeStruct((nbatch * n2, n1 * width), BF16),
        compiler_params=_params(32), name="four3")(
            zr.reshape(nbatch * n1, n2, width), zi.reshape(nbatch * n1, n2, width), m3, mc)
    return f.reshape(nbatch * seq, width)


def _proj_kernel(a_ref, fp_ref, fs_ref, g_ref, wa_ref, wf_ref, bf_ref, o_ref, *, n_p, d):
    i = pl.program_id(0)
    fm = jnp.where(i < n_p, fp_ref[...], fs_ref[...])
    a = jnp.dot(a_ref[...], wa_ref[...], preferred_element_type=F32)
    f = jnp.dot(fm, wf_ref[...], preferred_element_type=F32) + bf_ref[...]
    ga = g_ref[:, :d].astype(F32)
    gf = g_ref[:, d:].astype(F32)
    o_ref[...] = (jax.nn.sigmoid(ga) * a + jax.nn.sigmoid(gf) * f).astype(o_ref.dtype)


def _proj(cfg, attn, f_p, f_s, gates, wa, wf, bf):
    tm, d, aw, fw = cfg.tm_proj, cfg.d_model, cfg.attn_width, cfg.fourier_width
    const = lambda shape: pl.BlockSpec(shape, lambda i: (0, 0))
    vmem = (2 * (aw + fw) * d * 2 + 2 * tm * (aw + 2 * fw + 3 * d) * 2 + 3 * tm * d * 4) / MIB + 8
    return pl.pallas_call(
        functools.partial(_proj_kernel, n_p=cfg.tokens_p // tm, d=d), grid=(cfg.tokens // tm,),
        in_specs=[pl.BlockSpec((tm, aw), lambda i: (i, 0))] + _two_group_specs(cfg, tm, fw) + [
            pl.BlockSpec((tm, 2 * d), lambda i: (i, 0)), const((aw, d)), const((fw, d)), const((1, d))],
        out_specs=pl.BlockSpec((tm, d), lambda i: (i, 0)),
        out_shape=jax.ShapeDtypeStruct((cfg.tokens, d), BF16),
        compiler_params=_params(vmem), name="proj")(attn, f_p, f_s, gates, wa, wf, bf)


def _resid_kernel(m_ref, wo_ref, xp_ref, xs_ref, gate_ref, o_ref, *, n_p):
    i = pl.program_id(0)
    x = jnp.where(i < n_p, xp_ref[...], xs_ref[...])
    o_ref[...] = x + gate_ref[...] * jnp.dot(m_ref[...], wo_ref[...], preferred_element_type=F32)


def _resid(cfg, mixed, wo, x_p, x_s, ada3):
    tm, d = cfg.tm_proj, cfg.d_model
    vmem = (2 * d * d * 2 + 2 * tm * d * 2 + 7 * tm * d * 4) / MIB + 8
    return pl.pallas_call(
        functools.partial(_resid_kernel, n_p=cfg.tokens_p // tm), grid=(cfg.tokens // tm,),
        in_specs=[pl.BlockSpec((tm, d), lambda i: (i, 0)), pl.BlockSpec((d, d), lambda i: (0, 0))]
        + _two_group_specs(cfg, tm, d) + [_ada_spec(cfg, tm, 2)],
        out_specs=pl.BlockSpec((tm, d), lambda i: (i, 0)),
        out_shape=jax.ShapeDtypeStruct((cfg.tokens, d), F32),
        compiler_params=_params(vmem), name="resid")(mixed, wo, x_p, x_s, ada3)


def _router_kernel(x_ref, g_ref, sh_ref, sc_ref, wrt_ref, rb_ref, tri_ref,
                   hp_ref, idx_ref, wgt_ref, pos_ref, cnt_ref, carry_ref, *, cfg):
    d, ne, ng = cfg.d_model, cfg.n_experts, cfg.n_egroups
    per = ne // ng
    i = pl.program_id(0)

    @pl.when(i == 0)
    def _():
        carry_ref[...] = jnp.zeros_like(carry_ref)

    h = _modulated_norm(x_ref[...], g_ref[...], sh_ref[...], sc_ref[...])
    tm = h.shape[0]
    _store_rows(hp_ref, 0, _pack_bf16_pair(h[:, :d // 2], h[:, d // 2:]))

    logits = lax.dot_general(wrt_ref[...], h, (((1,), (1,)), ((), ())), preferred_element_type=F32,
                             precision=lax.Precision.HIGHEST)
    scores = jax.nn.sigmoid(logits)
    sel = scores + rb_ref[...]
    s3 = sel.reshape(ng, per, tm)
    io3 = lax.broadcasted_iota(I32, (ng, per, tm), 1)
    m1 = jnp.max(s3, axis=1, keepdims=True)
    i1 = jnp.min(jnp.where(s3 == m1, io3, per), axis=1, keepdims=True)
    m2 = jnp.max(jnp.where(io3 == i1, -jnp.inf, s3), axis=1, keepdims=True)
    gs = (m1 + m2).reshape(ng, tm)
    gio = lax.broadcasted_iota(I32, (ng, tm), 0)
    before = jnp.zeros((ng, tm), I32)
    for g2 in range(ng):
        row = gs[g2:g2 + 1, :]
        ahead = jnp.where(row > gs, 1, jnp.where(row == gs, jnp.where(gio > g2, 1, 0), 0))
        before = before + ahead
    keep = jnp.where(before < cfg.topk_groups, 1.0, 0.0)
    keep_e = jnp.broadcast_to(keep.reshape(ng, 1, tm), (ng, per, tm)).reshape(ne, tm)
    cur = jnp.where(keep_e > 0.5, sel, -jnp.inf)

    eio = lax.broadcasted_iota(I32, (ne, tm), 0)
    chosen, picked = [], jnp.zeros((ne, tm), F32)
    for _ in range(cfg.top_k):
        m = jnp.max(cur, axis=0, keepdims=True)
        ij = jnp.min(jnp.where(cur == m, eio, ne), axis=0, keepdims=True)
        hit = eio == ij
        picked = jnp.where(hit, 1.0, picked)
        cur = jnp.where(hit, -jnp.inf, cur)
        chosen.append(ij)
    before_t = jnp.dot(picked.astype(BF16), tri_ref[...], preferred_element_type=F32)
    rank = carry_ref[...] + before_t
    sc_rows, pos_rows = [], []
    for ij in chosen:
        hit = eio == ij
        sc_rows.append(jnp.sum(jnp.where(hit, scores, 0.0), axis=0, keepdims=True))
        pos_rows.append(jnp.sum(jnp.where(hit, rank, 0.0), axis=0, keepdims=True))
    sc_all = jnp.concatenate(sc_rows, axis=0)
    total = jnp.sum(sc_all, axis=0, keepdims=True)
    idx_ref[...] = jnp.concatenate(chosen, axis=0)
    wgt_ref[...] = sc_all / total * cfg.routed_scale
    pos_ref[...] = jnp.concatenate(pos_rows, axis=0).astype(I32)
    carry_ref[...] = carry_ref[...] + jnp.sum(picked, axis=1, keepdims=True)
    cnt_ref[...] = carry_ref[...].astype(I32)


def _router(cfg, x1, g, ada3, w_router, router_bias):
    tm, d, ne, k = cfg.tm_router, cfg.d_model, cfg.n_experts, cfg.top_k
    t = cfg.tokens
    tri = jnp.asarray(np.triu(np.ones((tm, tm), np.float32), 1), BF16)
    row = lambda dt: jax.ShapeDtypeStruct((k, t), dt)
    row_spec = pl.BlockSpec((k, tm), lambda i: (0, i))
    return pl.pallas_call(
        functools.partial(_router_kernel, cfg=cfg), grid=(t // tm,),
        in_specs=[pl.BlockSpec((tm, d), lambda i: (i, 0)), pl.BlockSpec((1, d), lambda i: (0, 0)),
                  _ada_spec(cfg, tm, 3), _ada_spec(cfg, tm, 4),
                  pl.BlockSpec((ne, d), lambda i: (0, 0)), pl.BlockSpec((ne, 1), lambda i: (0, 0)),
                  pl.BlockSpec((tm, tm), lambda i: (0, 0))],
        out_specs=[pl.BlockSpec((tm * _lines(d // 2), LANES), lambda i: (i, 0)), row_spec, row_spec, row_spec,
                   pl.BlockSpec((ne, 1), lambda i: (0, 0))],
        out_shape=[jax.ShapeDtypeStruct((t * _lines(d // 2), LANES), U32), row(I32), row(F32), row(I32),
                   jax.ShapeDtypeStruct((ne, 1), I32)],
        scratch_shapes=[pltpu.VMEM((ne, 1), F32)],
        compiler_params=_params(40), name="router")(
            x1, g, ada3, ada3, w_router.T.astype(F32), router_bias.reshape(ne, 1).astype(F32), tri)


def _slots_kernel(pstart_ref, idx_ref, pos_ref, o_ref, *, n_experts):
    idx = idx_ref[...]
    base = lax.fori_loop(0, n_experts, lambda e, acc: jnp.where(idx == e, pstart_ref[e], acc),
                         jnp.zeros(idx.shape, I32))
    o_ref[...] = base + pos_ref[...]


def _slots(cfg, pstart, idx, pos):
    k, t = idx.shape
    tn = math.gcd(t, 4096)
    spec = pl.BlockSpec((k, tn), lambda i, *_: (0, i))
    grid_spec = pltpu.PrefetchScalarGridSpec(num_scalar_prefetch=1, grid=(t // tn,), in_specs=[spec, spec],
                                             out_specs=spec)
    return pl.pallas_call(functools.partial(_slots_kernel, n_experts=cfg.n_experts), grid_spec=grid_spec,
                          out_shape=jax.ShapeDtypeStruct((k, t), I32), name="slots")(pstart, idx, pos)


def _dispatch_kernel(pend_ref, padded_ref, nt_ref, hp_ref, slot_ref, xs_ref, zero_ref, sem, zsem, *, cfg):
    tm, tme, k, ne = cfg.tm_dispatch, cfg.tm_expert, cfg.top_k, cfg.n_experts
    lines = _lines(cfg.d_model // 2)
    n_tiles = xs_ref.shape[0] // (tme * lines)
    i = pl.program_id(0)

    def pad_copy(e):
        return pltpu.make_async_copy(zero_ref, xs_ref.at[_row_window(pend_ref[e] - tme, tme, lines)], zsem)

    def tail_copy(t):
        return pltpu.make_async_copy(zero_ref, xs_ref.at[_row_window(t * tme, tme, lines)], zsem)

    @pl.when(i == 0)
    def _():
        zero_ref[...] = jnp.zeros_like(zero_ref)

        def start(e, c):
            pl.when(padded_ref[e] > 0)(lambda: pad_copy(e).start())
            return c

        def wait(e, c):
            pl.when(padded_ref[e] > 0)(lambda: pad_copy(e).wait())
            return c

        lax.fori_loop(0, ne, start, 0)
        lax.fori_loop(nt_ref[0], n_tiles, lambda t, c: (tail_copy(t).start(), c)[1], 0)
        lax.fori_loop(0, ne, wait, 0)
        lax.fori_loop(nt_ref[0], n_tiles, lambda t, c: (tail_copy(t).wait(), c)[1], 0)

    def row(r, c):
        for j in range(k):
            pltpu.make_async_copy(hp_ref.at[_row_window(r, 1, lines)],
                                  xs_ref.at[_row_window(slot_ref[j, r], 1, lines)], sem).start(priority=j % 2)
        return c

    lax.fori_loop(0, tm, row, 0)
    for _ in range(k):
        pltpu.make_async_copy(hp_ref, xs_ref.at[_row_window(0, tm, lines)], sem).wait()


def _dispatch(cfg, hp, slot, pend, padded, n_used, n_slots):
    tm, d, k = cfg.tm_dispatch, cfg.d_model, cfg.top_k
    lines = _lines(d // 2)
    grid_spec = pltpu.PrefetchScalarGridSpec(
        num_scalar_prefetch=3, grid=(cfg.tokens // tm,),
        in_specs=[pl.BlockSpec((tm * lines, LANES), lambda i, *_: (i, 0)),
                  pl.BlockSpec((k, tm), lambda i, *_: (0, i), memory_space=pltpu.SMEM)],
        out_specs=pl.BlockSpec(memory_space=pl.ANY),
        scratch_shapes=[pltpu.VMEM((cfg.tm_expert * lines, LANES), U32), pltpu.SemaphoreType.DMA,
                        pltpu.SemaphoreType.DMA])
    return pl.pallas_call(
        functools.partial(_dispatch_kernel, cfg=cfg), grid_spec=grid_spec,
        out_shape=jax.ShapeDtypeStruct((n_slots * lines, LANES), U32),
        compiler_params=_params(24), name="dispatch")(pend, padded, n_used, hp, slot)


def _expert_kernel(te_ref, nt_ref, nxt_ref, xs_ref, wg_hbm, wu_hbm, wd_hbm, ys_ref,
                   wgf, wuf, wdf, wgb, wub, wdb, side_ref, wsem):
    i = pl.program_id(0)
    nt = nt_ref[0]
    tme = xs_ref.shape[0] // _lines(wdb.shape[1] // 2)
    half = wdb.shape[1] // 2

    def weight_copies(e, s):
        return [pltpu.make_async_copy(src.at[e], dst.at[s], wsem.at[s])
                for src, dst in ((wg_hbm, wgf), (wu_hbm, wuf), (wd_hbm, wdf))]

    @pl.when(i >= nt)
    def _():
        ys_ref[...] = jnp.zeros_like(ys_ref)

    @pl.when(i < nt)
    def _():
        e = te_ref[i]

        @pl.when(i == 0)
        def _():
            side_ref[0] = 0
            for c in weight_copies(e, 0):
                c.start()

        @pl.when(jnp.logical_or(i == 0, e != te_ref[jnp.maximum(i - 1, 0)]))
        def _():
            s = side_ref[0]
            nxt = nxt_ref[e]
            @pl.when(nxt >= 0)
            def _():
                for c in weight_copies(nxt, 1 - s):
                    c.start()

            for c in weight_copies(e, s):
                c.wait()
            wgb[...] = wgf[s].astype(BF16)
            wub[...] = wuf[s].astype(BF16)
            wdb[...] = wdf[s].astype(BF16)
            side_ref[0] = 1 - s

        lo, hi = _unpack_bf16_pair(_load_rows(xs_ref, 0, tme, half))
        x = jnp.concatenate([lo.astype(BF16), hi.astype(BF16)], axis=1)
        g = jnp.dot(x, wgb[...], preferred_element_type=F32)
        u = jnp.dot(x, wub[...], preferred_element_type=F32)
        a = (g * jax.nn.sigmoid(g) * u).astype(BF16)
        cb = _pair_block(2 * half)
        lines = _lines(half)
        for m in range(half // cb):
            ym = jnp.dot(a, wdb[:, 2 * m * cb:(2 * m + 2) * cb], preferred_element_type=F32)
            packed = _pack_bf16_pair(ym[:, :cb], ym[:, cb:])
            for c in range(cb // LANES):
                ys_ref[pl.ds(m * (cb // LANES) + c, tme, stride=lines), :] = packed[:, c * LANES:(c + 1) * LANES]


def _experts(cfg, xs, tile_expert, n_used, next_expert, wg, wu, wd):
    tme, d, de = cfg.tm_expert, cfg.d_model, cfg.d_expert
    lines = _lines(d // 2)
    n_tiles = xs.shape[0] // (tme * lines)
    any_spec = pl.BlockSpec(memory_space=pl.ANY)
    grid_spec = pltpu.PrefetchScalarGridSpec(
        num_scalar_prefetch=3, grid=(n_tiles,),
        in_specs=[pl.BlockSpec((tme * lines, LANES), lambda i, te, nt, nx: (jnp.minimum(i, nt[0] - 1), 0)),
                  any_spec, any_spec, any_spec],
        out_specs=pl.BlockSpec((tme * lines, LANES), lambda i, te, nt, nx: (i, 0)),
        scratch_shapes=[pltpu.VMEM((2, d, de), F32), pltpu.VMEM((2, d, de), F32), pltpu.VMEM((2, de, d), F32),
                        pltpu.VMEM((d, de), BF16), pltpu.VMEM((d, de), BF16), pltpu.VMEM((de, d), BF16),
                        pltpu.SMEM((1,), I32), pltpu.SemaphoreType.DMA((2,))])
    vmem = (2 * 3 * d * de * 4 + 3 * d * de * 2 + 6 * tme * d * 2 + 6 * tme * d * 4) / MIB + 8
    return pl.pallas_call(
        _expert_kernel, grid_spec=grid_spec, out_shape=jax.ShapeDtypeStruct(xs.shape, U32),
        compiler_params=_params(vmem), name="experts")(tile_expert, n_used, next_expert, xs, wg, wu, wd)


def _final_kernel(slot_ref, slotn_ref, x_ref, wt_ref, g_ref, sh_ref, sc_ref, gate_ref, wg_ref, wu_ref, wd_ref,
                  nf_ref, ys_ref, o_ref, buf0, buf1, sem, *, cfg):
    tm, k = cfg.tm_final, cfg.top_k
    half = cfg.d_model // 2
    lines = _lines(half)
    bufs = (buf0, buf1)
    i = pl.program_id(0)
    n = pl.num_programs(0)

    def row_copy(slots, b, j, r):
        return pltpu.make_async_copy(ys_ref.at[_row_window(slots[j, r], 1, lines)],
                                     bufs[b].at[_row_window(j * tm + r, 1, lines)], sem.at[b])

    def wait_planes(b):
        for j in range(k):
            pltpu.make_async_copy(ys_ref.at[_row_window(0, tm, lines)],
                                  bufs[b].at[_row_window(j * tm, tm, lines)], sem.at[b]).wait()

    @pl.when(i == 0)
    def _():
        def row(r, c):
            for j in range(k):
                row_copy(slot_ref, 0, j, r).start(priority=j % 2)
            return c

        lax.fori_loop(0, tm, row, 0)

    def step(b):
        wait_planes(b)
        for r in range(tm):
            for j in range(k):
                row_copy(slotn_ref, 1 - b, j, r).start(priority=j % 2)

        x = x_ref[...]
        h = _modulated_norm(x, g_ref[...], sh_ref[...], sc_ref[...]).astype(BF16)
        g = jnp.dot(h, wg_ref[...], preferred_element_type=F32)
        u = jnp.dot(h, wu_ref[...], preferred_element_type=F32)
        y = jnp.dot((g * jax.nn.sigmoid(g) * u).astype(BF16), wd_ref[...], preferred_element_type=F32)
        cb = _pair_block(2 * half)
        for j in range(k):
            lo, hi = _unpack_bf16_pair(_load_rows(bufs[b], j * tm, tm, half))
            blocks = [part[:, m * cb:(m + 1) * cb] for m in range(half // cb) for part in (lo, hi)]
            y = y + wt_ref[:, j:j + 1] * jnp.concatenate(blocks, axis=1)
        xo = x + gate_ref[...] * y
        o_ref[...] = xo * lax.rsqrt(jnp.mean(xo * xo, axis=-1, keepdims=True) + EPS) * nf_ref[...]
        pl.when(i == n - 1)(lambda: wait_planes(1 - b))

    for b in range(2):
        pl.when(i % 2 == b)(functools.partial(step, b))


def _final_group(cfg, tile_off, batch_off, nbatch, seq, x1, ys, slot, wgt_t, ada3, g, wgs, wus, wds, nf):
    tm, d, k, ds = cfg.tm_final, cfg.d_model, cfg.top_k, cfg.d_shared
    n = nbatch * seq // tm
    per_batch = seq // tm
    ada = lambda which: pl.BlockSpec((None, 1, d), lambda i: ((batch_off + i // per_batch) * 6 + which, 0, 0))
    const = lambda shape: pl.BlockSpec(shape, lambda i: (0, 0))
    vmem = (2 * k * tm * d * 2 + 2 * 3 * d * ds * 2 + 10 * tm * d * 4) / MIB + 8
    return pl.pallas_call(
        functools.partial(_final_kernel, cfg=cfg), grid=(n,),
        in_specs=[pl.BlockSpec((k, tm), lambda i: (0, tile_off + i), memory_space=pltpu.SMEM),
                  pl.BlockSpec((k, tm), lambda i: (0, tile_off + jnp.minimum(i + 1, n - 1)), memory_space=pltpu.SMEM),
                  pl.BlockSpec((tm, d), lambda i: (tile_off + i, 0)),
                  pl.BlockSpec((tm, k), lambda i: (tile_off + i, 0)),
                  const((1, d)), ada(3), ada(4), ada(5), const((d, ds)), const((d, ds)), const((ds, d)),
                  const((1, d)), pl.BlockSpec(memory_space=pl.ANY)],
        out_specs=pl.BlockSpec((tm, d), lambda i: (i, 0)),
        out_shape=jax.ShapeDtypeStruct((nbatch * seq, d), F32),
        scratch_shapes=[pltpu.VMEM((k * tm * _lines(d // 2), LANES), U32),
                        pltpu.VMEM((k * tm * _lines(d // 2), LANES), U32), pltpu.SemaphoreType.DMA((2,))],
        compiler_params=_params(vmem), name="final")(
            slot, slot, x1, wgt_t, g, ada3, ada3, ada3, wgs, wus, wds, nf, ys)


def _forward(cfg, x_prompt, x_sample, c_prompt, c_sample, w_ada, b_ada, norm_mix, w_in, rel_bias, sink,
             w_attn_proj, w_four, b_four, w_out, norm_ffn, w_router, router_bias, w_gate_e, w_up_e, w_down_e,
             w_gate_s, w_up_s, w_down_s, norm_final):
    d, t = cfg.d_model, cfg.tokens
    aw, kvw, fw = cfg.attn_width, cfg.kv_width, cfg.fourier_width
    x_p = x_prompt.reshape(cfg.tokens_p, d)
    x_s = x_sample.reshape(cfg.tokens_s, d)

    nc = cfg.batch + cfg.dec_batch
    rows = -(-nc // 8) * 8
    c_all = jnp.concatenate([c_prompt, c_sample, jnp.zeros((rows - nc, d), F32)], axis=0)
    ada3 = _ada(cfg, c_all, w_ada[0], b_ada).reshape(rows * 6, 1, d)

    h = _modnorm(cfg, x_p, x_s, norm_mix, ada3)
    w_in_b = w_in[0].astype(BF16)
    c0, c1 = aw + 2 * kvw, aw + 2 * kvw + fw
    qkv = _matmul(h, w_in_b[:, :c0], cfg.tm_mm, c0, "mm_qkv")
    u = _matmul(h, w_in_b[:, c0:c1], cfg.tm_mm, fw, "mm_u")
    gates = _matmul(h, w_in_b[:, c1:], cfg.tm_mm, d, "mm_gates")

    attn = _attention(cfg, qkv, _bias_table(cfg, rel_bias), sink[0])
    u2 = u.reshape(t // cfg.four_n2, cfg.four_n2 * fw)
    f_p = _fourier_group(cfg, u2, 0, cfg.batch, cfg.seq)
    f_s = _fourier_group(cfg, u2, cfg.tokens_p // cfg.four_n2, cfg.dec_batch, cfg.dec_seq)
    mixed = _proj(cfg, attn, f_p, f_s, gates, w_attn_proj[0].astype(BF16), w_four[0].astype(BF16), b_four)
    x1 = _resid(cfg, mixed, w_out[0].astype(BF16), x_p, x_s, ada3)

    hp, idx, wgt, pos, cnt = _router(cfg, x1, norm_ffn, ada3, w_router[0], router_bias[0])
    tme, ne = cfg.tm_expert, cfg.n_experts
    counts = cnt[:, 0]
    padded = (counts + tme - 1) // tme * tme
    pend = jnp.cumsum(padded).astype(I32)
    slot = _slots(cfg, pend - padded, idx, pos)
    n_tiles = t * cfg.top_k // tme + ne
    tile_start = jnp.arange(n_tiles, dtype=I32) * tme
    tile_expert = jnp.minimum(jnp.sum((pend[None, :] <= tile_start[:, None]).astype(I32), axis=1), ne - 1)
    n_used = (pend[-1:] // tme).astype(I32)
    owner = jnp.where(counts > 0, jnp.arange(ne, dtype=I32), ne)
    later = jnp.concatenate([lax.cummin(owner, reverse=True)[1:], jnp.full((1,), ne, I32)])
    next_expert = jnp.where(later < ne, later, -1).astype(I32)
    xs = _dispatch(cfg, hp, slot, pend, padded, n_used, n_tiles * tme)
    ys = _experts(cfg, xs, tile_expert, n_used, next_expert, w_gate_e[0], w_up_e[0], w_down_e[0])

    wgt_t = wgt.T
    shared = (w_gate_s[0].astype(BF16), w_up_s[0].astype(BF16), w_down_s[0].astype(BF16))
    nf = norm_final.reshape(1, d)
    y_p = _final_group(cfg, 0, 0, cfg.batch, cfg.seq, x1, ys, slot, wgt_t, ada3, norm_ffn, *shared, nf)
    y_s = _final_group(cfg, cfg.tokens_p // cfg.tm_final, cfg.batch, cfg.dec_batch, cfg.dec_seq, x1, ys, slot,
                       wgt_t, ada3, norm_ffn, *shared, nf)
    return (y_p.reshape(cfg.batch, cfg.seq, d), y_s.reshape(cfg.dec_batch, cfg.dec_seq, d))


def kernel(x_prompt, x_sample, c_prompt, c_sample, w_ada, b_ada, norm_mix, w_in, rel_bias, sink, w_attn_proj,
           w_four, b_four, w_out, norm_ffn, w_router, router_bias, w_gate_e, w_up_e, w_down_e, w_gate_s, w_up_s,
           w_down_s, norm_final):
    return _forward(Cfg(), x_prompt, x_sample, c_prompt, c_sample, w_ada, b_ada, norm_mix, w_in, rel_bias, sink,
                    w_attn_proj, w_four, b_four, w_out, norm_ffn, w_router, router_bias, w_gate_e, w_up_e,
                    w_down_e, w_gate_s, w_up_s, w_down_s, norm_final)
```

```python
import functools
import math
from typing import NamedTuple

import jax
import jax.numpy as jnp
import numpy as np
from jax import lax
from jax.experimental import pallas as pl
from jax.experimental.pallas import tpu as pltpu

F32 = jnp.float32
BF16 = jnp.bfloat16
I32 = jnp.int32
U32 = jnp.uint32

EPS = 1e-6
NEG_INF = -1e30
LANES = 128
MIB = 1024 * 1024


class Cfg(NamedTuple):
    d_model: int = 2048
    batch: int = 4
    seq: int = 8192
    dec_batch: int = 8
    dec_seq: int = 2048
    n_heads: int = 32
    n_kv: int = 4
    head_dim: int = 64
    window: int = 128
    n_buckets: int = 32
    max_distance: int = 128
    n_fgroups: int = 8
    fgroup: int = 128
    n_experts: int = 256
    top_k: int = 8
    n_egroups: int = 8
    topk_groups: int = 4
    d_expert: int = 512
    d_shared: int = 512
    routed_scale: float = 2.5
    tm_norm: int = 512
    tm_mm: int = 512
    tm_proj: int = 256
    tm_router: int = 256
    tm_dispatch: int = 256
    tm_expert: int = 256
    tm_final: int = 256
    four_n2: int = 128
    four_t2: int = 8

    @property
    def attn_width(self):
        return self.n_heads * self.head_dim

    @property
    def kv_width(self):
        return self.n_kv * self.head_dim

    @property
    def fourier_width(self):
        return self.n_fgroups * self.fgroup

    @property
    def tokens_p(self):
        return self.batch * self.seq

    @property
    def tokens_s(self):
        return self.dec_batch * self.dec_seq

    @property
    def tokens(self):
        return self.tokens_p + self.tokens_s


def _params(vmem_mib):
    return pltpu.CompilerParams(vmem_limit_bytes=int(vmem_mib * MIB))


def _batch_of_tile(i, cfg, tm):
    n_p = cfg.tokens_p // tm
    return jnp.where(i < n_p, i // (cfg.seq // tm), cfg.batch + (i - n_p) // (cfg.dec_seq // tm))


def _ada_spec(cfg, tm, which):
    return pl.BlockSpec((None, 1, cfg.d_model), lambda i: (_batch_of_tile(i, cfg, tm) * 6 + which, 0, 0))


def _two_group_specs(cfg, tm, width):
    n_p = cfg.tokens_p // tm
    return [pl.BlockSpec((tm, width), lambda i: (jnp.minimum(i, n_p - 1), 0)),
            pl.BlockSpec((tm, width), lambda i: (jnp.maximum(i - n_p, 0), 0))]


def _modulated_norm(x, g, shift, scale):
    y = x * lax.rsqrt(jnp.mean(x * x, axis=-1, keepdims=True) + EPS) * g
    return y * (1.0 + scale) + shift


def _pack_bf16_pair(lo, hi):
    lo_bits = pltpu.bitcast(lo.astype(BF16).astype(F32), U32) >> 16
    hi_bits = pltpu.bitcast(hi.astype(BF16).astype(F32), U32) & jnp.uint32(0xFFFF0000)
    return hi_bits | lo_bits


def _unpack_bf16_pair(p):
    lo = pltpu.bitcast(p << 16, F32)
    hi = pltpu.bitcast(p & jnp.uint32(0xFFFF0000), F32)
    return lo, hi


def _pair_block(d):
    return min(256, d // 2)


def _lines(width):
    return width // LANES


def _row_window(row, n_rows, lines):
    return pl.ds(pl.multiple_of(row * lines, lines), n_rows * lines)


def _load_rows(ref, row, n_rows, width):
    lines = _lines(width)
    base = pl.multiple_of(row * lines, lines) if not isinstance(row, int) else row * lines
    return jnp.concatenate([ref[pl.ds(base + c, n_rows, stride=lines), :] for c in range(lines)], axis=1)


def _store_rows(ref, row, mat):
    n_rows, width = mat.shape
    lines = _lines(width)
    for c in range(lines):
        ref[pl.ds(row * lines + c, n_rows, stride=lines), :] = mat[:, c * LANES:(c + 1) * LANES]


def _ada_kernel(c_ref, w_ref, b_ref, o_ref):
    c = c_ref[...]
    s = c * jax.nn.sigmoid(c)
    o_ref[...] = jnp.dot(s, w_ref[...], preferred_element_type=F32, precision=lax.Precision.HIGHEST) + b_ref[...]


def _ada(cfg, c_all, w_ada, b_ada):
    rows, d = c_all.shape
    n = w_ada.shape[1]
    tn = n // 8
    return pl.pallas_call(
        _ada_kernel, grid=(n // tn,),
        in_specs=[pl.BlockSpec((rows, d), lambda j: (0, 0)), pl.BlockSpec((d, tn), lambda j: (0, j)),
                  pl.BlockSpec((1, tn), lambda j: (0, j))],
        out_specs=pl.BlockSpec((rows, tn), lambda j: (0, j)),
        out_shape=jax.ShapeDtypeStruct((rows, n), F32),
        compiler_params=_params(2 * d * tn * 4 / MIB + 8), name="ada")(c_all, w_ada, b_ada)


def _modnorm_kernel(xp_ref, xs_ref, g_ref, sh_ref, sc_ref, o_ref, *, n_p):
    i = pl.program_id(0)

    def run(x_ref):
        o_ref[...] = _modulated_norm(x_ref[...], g_ref[...], sh_ref[...], sc_ref[...]).astype(o_ref.dtype)

    pl.when(i < n_p)(lambda: run(xp_ref))
    pl.when(i >= n_p)(lambda: run(xs_ref))


def _modnorm(cfg, x_p, x_s, g, ada3):
    tm, d = cfg.tm_norm, cfg.d_model
    return pl.pallas_call(
        functools.partial(_modnorm_kernel, n_p=cfg.tokens_p // tm), grid=(cfg.tokens // tm,),
        in_specs=_two_group_specs(cfg, tm, d) + [pl.BlockSpec((1, d), lambda i: (0, 0)),
                                                 _ada_spec(cfg, tm, 0), _ada_spec(cfg, tm, 1)],
        out_specs=pl.BlockSpec((tm, d), lambda i: (i, 0)),
        out_shape=jax.ShapeDtypeStruct((cfg.tokens, d), BF16),
        compiler_params=_params(7 * tm * d * 4 / MIB + 8), name="modnorm")(x_p, x_s, g, ada3, ada3)


def _mm_kernel(a_ref, w_ref, o_ref):
    o_ref[...] = jnp.dot(a_ref[...], w_ref[...], preferred_element_type=F32).astype(o_ref.dtype)


def _matmul(a, w, tm, tn, name):
    m, k = a.shape
    n = w.shape[1]
    vmem = (2 * tm * k * 2 + 2 * k * tn * 2 + 2 * tm * tn * 2 + tm * tn * 4) / MIB + 8
    return pl.pallas_call(
        _mm_kernel, grid=(n // tn, m // tm),
        in_specs=[pl.BlockSpec((tm, k), lambda j, i: (i, 0)), pl.BlockSpec((k, tn), lambda j, i: (0, j))],
        out_specs=pl.BlockSpec((tm, tn), lambda j, i: (i, j)),
        out_shape=jax.ShapeDtypeStruct((m, n), BF16),
        compiler_params=_params(vmem), name=name)(a, w)


def _t5_bucket_table(cfg):
    blk, span = cfg.window, 3 * cfg.window
    rel = (np.arange(span)[None, :] - cfg.window - np.arange(blk)[:, None]).astype(np.int32)
    nb = cfg.n_buckets // 2
    max_exact = nb // 2
    ret = np.where(rel > 0, nb, 0)
    n = np.abs(rel)
    nf = np.maximum(n, 1).astype(np.float32)
    ratio = np.log(nf / np.float32(max_exact)) / np.float32(math.log(cfg.max_distance / max_exact))
    large = max_exact + (ratio * np.float32(nb - max_exact)).astype(np.int32)
    large = np.minimum(large, nb - 1)
    return (ret + np.where(n < max_exact, n, large)).astype(np.int32), rel


def _bias_kernel(bucket_ref, rbt_ref, o_ref, *, n_buckets):
    bucket = bucket_ref[...]
    acc = jnp.zeros(o_ref.shape, F32)
    for b in range(n_buckets):
        acc = jnp.where(bucket == b, rbt_ref[:, b:b + 1], acc)
    o_ref[...] = acc


def _bias_table(cfg, rel_bias):
    buckets, _ = _t5_bucket_table(cfg)
    blk, span, h = cfg.window, 3 * cfg.window, cfg.n_heads
    n = blk * span
    flat = pl.pallas_call(
        functools.partial(_bias_kernel, n_buckets=cfg.n_buckets),
        out_shape=jax.ShapeDtypeStruct((h, n), F32), name="bias")(
            jnp.asarray(buckets.reshape(1, n)), rel_bias.T.astype(F32))
    pairs = h // cfg.n_kv // 2
    t = flat.reshape(cfg.n_kv, pairs, 2, blk, span)
    return jnp.transpose(t, (0, 1, 3, 2, 4)).reshape(cfg.n_kv, pairs * blk, 2 * span)


def _attn_kernel(q_ref, kp_ref, kc_ref, kn_ref, vp_ref, vc_ref, vn_ref, bias_ref, sink_ref, rel_ref, kofs_ref,
                 ones_ref, o_ref, valid_ref, s_ref, l_ref, p_ref, m_ref, *, cfg, blocks_p):
    blk, hd = cfg.window, cfg.head_dim
    span = 3 * blk
    pairs = cfg.n_heads // cfg.n_kv // 2
    rc = 64
    nb_p, nb_s = cfg.seq // blk, cfg.dec_seq // blk
    i = pl.program_id(0)
    in_p = i < blocks_p
    pos = jnp.where(in_p, i % nb_p, (i - blocks_p) % nb_s)
    seq_len = jnp.where(in_p, cfg.seq, cfg.dec_seq)
    at_edge = jnp.logical_or(pos == 0, (pos + 1) * blk == seq_len)

    @pl.when(at_edge)
    def _():
        kpos = pos * blk + kofs_ref[...]
        in_band = jnp.abs(rel_ref[...]) <= cfg.window
        in_seq = (kpos >= 0) & (kpos < seq_len)
        valid_ref[...] = jnp.where(in_band, jnp.where(in_seq, 1.0, 0.0), 0.0)

    @pl.when(jnp.logical_not(at_edge))
    def _():
        valid_ref[...] = jnp.where(jnp.abs(rel_ref[...]) <= cfg.window, 1.0, 0.0)

    k = jnp.concatenate([kp_ref[...], kc_ref[...], kn_ref[...]], axis=0)
    v = jnp.concatenate([vp_ref[...], vc_ref[...], vn_ref[...]], axis=0)
    scale = hd ** -0.5
    lane = lax.broadcasted_iota(I32, (pairs * blk, 2 * hd), 1)
    for j in range(cfg.n_kv):
        kj = k[:, j * hd:(j + 1) * hd]
        vj = v[:, j * hd:(j + 1) * hd]
        zero = jnp.zeros_like(kj)
        k2 = jnp.concatenate([jnp.concatenate([kj, zero], axis=1), jnp.concatenate([zero, kj], axis=1)], axis=0)
        v2 = jnp.concatenate([jnp.concatenate([vj, zero], axis=1), jnp.concatenate([zero, vj], axis=1)], axis=0)
        base = j * pairs * 2 * hd
        qs = jnp.concatenate([q_ref[:, base + p * 2 * hd: base + (p + 1) * 2 * hd] for p in range(pairs)], axis=0)
        s_ref[...] = lax.dot_general(qs, k2, (((1,), (1,)), ((), ())), preferred_element_type=F32)
        units = [(slice(r0, r0 + rc), half) for r0 in range(0, pairs * blk, rc) for half in range(2)]

        for rows, half in units:
            cols = slice(half * span, (half + 1) * span)
            lh = s_ref[rows, cols] * scale + bias_ref[j, rows, cols]
            lh = jnp.where(valid_ref[rows, cols] > 0.5, lh, NEG_INF)
            l_ref[rows, cols] = lh
            sk = sink_ref[j, rows, half * hd:half * hd + 1]
            m = jnp.maximum(jnp.max(lh, axis=-1, keepdims=True), sk)
            m_ref[rows, half * LANES:(half + 1) * LANES] = jnp.broadcast_to(m, (rc, LANES))
        for rows, half in units:
            cols = slice(half * span, (half + 1) * span)
            mb = m_ref[rows, half * LANES:(half + 1) * LANES]
            p = jnp.exp(l_ref[rows, cols] - jnp.concatenate([mb] * (span // LANES), axis=1))
            p_ref[rows, cols] = p.astype(BF16)
        probs = p_ref[...]
        pv = jnp.dot(probs, v2, preferred_element_type=F32)
        psum = jnp.dot(probs, ones_ref[...], preferred_element_type=F32)
        m_lane = jnp.where(lane < hd, m_ref[:, :LANES], m_ref[:, LANES:])
        out = pv / (psum + jnp.exp(sink_ref[j] - m_lane))
        for p in range(pairs):
            o_ref[:, base + p * 2 * hd: base + (p + 1) * 2 * hd] = out[p * blk:(p + 1) * blk].astype(o_ref.dtype)


def _attention(cfg, qkv, bias_tbl, sink):
    blk, hd = cfg.window, cfg.head_dim
    aw, kvw = cfg.attn_width, cfg.kv_width
    pairs = cfg.n_heads // cfg.n_kv // 2
    nb_p, nb_s = cfg.seq // blk, cfg.dec_seq // blk
    blocks_p = cfg.batch * nb_p
    n_blocks = cfg.tokens // blk
    _, rel = _t5_bucket_table(cfg)
    rel_tbl = np.tile(rel, (pairs, 2)).astype(np.int32)
    kofs_tbl = np.tile((np.arange(3 * blk) - cfg.window)[None, :], (pairs * blk, 2)).astype(np.int32)
    sink_tbl = jnp.broadcast_to(sink.astype(F32).reshape(cfg.n_kv, pairs, 1, 2, 1),
                                (cfg.n_kv, pairs, blk, 2, hd)).reshape(cfg.n_kv, pairs * blk, 2 * hd)
    ones_tbl = jnp.asarray(np.kron(np.eye(2), np.ones((3 * blk, hd))), BF16)

    def seq_pos(i):
        in_p = i < blocks_p
        return jnp.where(in_p, i % nb_p, (i - blocks_p) % nb_s), jnp.where(in_p, nb_p, nb_s)

    def prev_blk(i):
        pos, _ = seq_pos(i)
        return jnp.where(pos == 0, i, i - 1)

    def next_blk(i):
        pos, nb = seq_pos(i)
        return jnp.where(pos == nb - 1, i, i + 1)

    kcol, vcol = aw // kvw, aw // kvw + 1
    full = lambda shape: pl.BlockSpec(shape, lambda i: (0,) * len(shape))
    return pl.pallas_call(
        functools.partial(_attn_kernel, cfg=cfg, blocks_p=blocks_p), grid=(n_blocks,),
        in_specs=[pl.BlockSpec((blk, aw), lambda i: (i, 0)),
                  pl.BlockSpec((blk, kvw), lambda i: (prev_blk(i), kcol)),
                  pl.BlockSpec((blk, kvw), lambda i: (i, kcol)),
                  pl.BlockSpec((blk, kvw), lambda i: (next_blk(i), kcol)),
                  pl.BlockSpec((blk, kvw), lambda i: (prev_blk(i), vcol)),
                  pl.BlockSpec((blk, kvw), lambda i: (i, vcol)),
                  pl.BlockSpec((blk, kvw), lambda i: (next_blk(i), vcol)),
                  full(bias_tbl.shape), full(sink_tbl.shape), full(rel_tbl.shape), full(kofs_tbl.shape),
                  full(ones_tbl.shape)],
        out_specs=pl.BlockSpec((blk, aw), lambda i: (i, 0)),
        out_shape=jax.ShapeDtypeStruct((cfg.tokens, aw), BF16),
        scratch_shapes=[pltpu.VMEM(rel_tbl.shape, F32), pltpu.VMEM(rel_tbl.shape, F32), pltpu.VMEM(rel_tbl.shape, F32),
                        pltpu.VMEM(rel_tbl.shape, BF16), pltpu.VMEM((pairs * blk, 2 * LANES), F32)],
        compiler_params=_params(40), name="attn")(
            qkv, qkv, qkv, qkv, qkv, qkv, qkv, bias_tbl, sink_tbl, jnp.asarray(rel_tbl), jnp.asarray(kofs_tbl),
            ones_tbl)


def _dft_cos_sin(n):
    ang = 2.0 * np.pi * np.outer(np.arange(n), np.arange(n)) / n
    return np.cos(ang), np.sin(ang)


def _four1_kernel(u_ref, m1_ref, twr_ref, twi_ref, zr_ref, zi_ref, *, n1, t2, width):
    y = jnp.dot(m1_ref[...], u_ref[...], preferred_element_type=F32)
    twr, twi = twr_ref[0], twi_ref[0]
    for l in range(t2):
        sl = slice(l * width, (l + 1) * width)
        yr, yi = y[:n1, sl], y[n1:, sl]
        cr, ci = twr[:, l:l + 1], twi[:, l:l + 1]
        zr_ref[:, sl] = (yr * cr - yi * ci).astype(zr_ref.dtype)
        zi_ref[:, sl] = (yr * ci + yi * cr).astype(zi_ref.dtype)


def _four3_kernel(zr_ref, zi_ref, m3_ref, mc_ref, o_ref, *, n2, n_groups, group, norm):
    z = jnp.concatenate([zr_ref[0], zi_ref[0]], axis=0)
    x = jnp.dot(m3_ref[...], z, preferred_element_type=F32)
    xr, xi = x[:n2].astype(BF16), x[n2:].astype(BF16)
    for g in range(n_groups):
        sl = slice(g * group, (g + 1) * group)
        xg = jnp.concatenate([xr[:, sl], xi[:, sl]], axis=1)
        o_ref[:, sl] = (jnp.dot(xg, mc_ref[...], preferred_element_type=F32) * norm).astype(o_ref.dtype)


def _fourier_group(cfg, u2, row_off, nbatch, seq):
    n2, t2, width = cfg.four_n2, cfg.four_t2, cfg.fourier_width
    n1 = seq // n2
    c1, s1 = _dft_cos_sin(n1)
    m1 = jnp.asarray(np.concatenate([c1, -s1], axis=0), BF16)
    ang = 2.0 * np.pi * np.outer(np.arange(n1), np.arange(n2)) / seq
    twr = jnp.asarray(np.cos(ang).reshape(n1, n2 // t2, t2).transpose(1, 0, 2), F32)
    twi = jnp.asarray((-np.sin(ang)).reshape(n1, n2 // t2, t2).transpose(1, 0, 2), F32)
    blk_off = row_off // n1
    zshape = jax.ShapeDtypeStruct((nbatch * n1, n2 * width), BF16)
    tw_spec = pl.BlockSpec((1, n1, t2), lambda b, j: (j, 0, 0))
    zr, zi = pl.pallas_call(
        functools.partial(_four1_kernel, n1=n1, t2=t2, width=width), grid=(nbatch, n2 // t2),
        in_specs=[pl.BlockSpec((n1, t2 * width), lambda b, j: (blk_off + b, j)),
                  pl.BlockSpec((2 * n1, n1), lambda b, j: (0, 0)), tw_spec, tw_spec],
        out_specs=[pl.BlockSpec((n1, t2 * width), lambda b, j: (b, j))] * 2,
        out_shape=[zshape, zshape],
        compiler_params=_params(32), name="four1")(u2, m1, twr, twi)

    c2, s2 = _dft_cos_sin(n2)
    m3 = jnp.asarray(np.block([[c2, s2], [-s2, c2]]), BF16)
    cc, sc = _dft_cos_sin(cfg.fgroup)
    mc = jnp.asarray(np.concatenate([cc, sc], axis=0), BF16)
    norm = 1.0 / math.sqrt(seq * cfg.fgroup)
    z_spec = pl.BlockSpec((1, n2, width), lambda b, k1: (b * n1 + k1, 0, 0))
    f = pl.pallas_call(
        functools.partial(_four3_kernel, n2=n2, n_groups=cfg.n_fgroups, group=cfg.fgroup, norm=norm),
        grid=(nbatch, n1),
        in_specs=[z_spec, z_spec, pl.BlockSpec((2 * n2, 2 * n2), lambda b, k1: (0, 0)),
                  pl.BlockSpec((2 * cfg.fgroup, cfg.fgroup), lambda b, k1: (0, 0))],
        out_specs=pl.BlockSpec((n2, width), lambda b, k1: (b, k1)),
        out_shape=jax.ShapeDtypeStruct((nbatch * n2, n1 * width), BF16),
        compiler_params=_params(32), name="four3")(
            zr.reshape(nbatch * n1, n2, width), zi.reshape(nbatch * n1, n2, width), m3, mc)
    return f.reshape(nbatch * seq, width)


def _proj_kernel(a_ref, fp_ref, fs_ref, g_ref, wa_ref, wf_ref, bf_ref, o_ref, *, n_p, d):
    i = pl.program_id(0)
    fm = jnp.where(i < n_p, fp_ref[...], fs_ref[...])
    a = jnp.dot(a_ref[...], wa_ref[...], preferred_element_type=F32)
    f = jnp.dot(fm, wf_ref[...], preferred_element_type=F32) + bf_ref[...]
    ga = g_ref[:, :d].astype(F32)
    gf = g_ref[:, d:].astype(F32)
    o_ref[...] = (jax.nn.sigmoid(ga) * a + jax.nn.sigmoid(gf) * f).astype(o_ref.dtype)


def _proj(cfg, attn, f_p, f_s, gates, wa, wf, bf):
    tm, d, aw, fw = cfg.tm_proj, cfg.d_model, cfg.attn_width, cfg.fourier_width
    const = lambda shape: pl.BlockSpec(shape, lambda i: (0, 0))
    vmem = (2 * (aw + fw) * d * 2 + 2 * tm * (aw + 2 * fw + 3 * d) * 2 + 3 * tm * d * 4) / MIB + 8
    return pl.pallas_call(
        functools.partial(_proj_kernel, n_p=cfg.tokens_p // tm, d=d), grid=(cfg.tokens // tm,),
        in_specs=[pl.BlockSpec((tm, aw), lambda i: (i, 0))] + _two_group_specs(cfg, tm, fw) + [
            pl.BlockSpec((tm, 2 * d), lambda i: (i, 0)), const((aw, d)), const((fw, d)), const((1, d))],
        out_specs=pl.BlockSpec((tm, d), lambda i: (i, 0)),
        out_shape=jax.ShapeDtypeStruct((cfg.tokens, d), BF16),
        compiler_params=_params(vmem), name="proj")(attn, f_p, f_s, gates, wa, wf, bf)


def _resid_kernel(m_ref, wo_ref, xp_ref, xs_ref, gate_ref, o_ref, *, n_p):
    i = pl.program_id(0)
    x = jnp.where(i < n_p, xp_ref[...], xs_ref[...])
    o_ref[...] = x + gate_ref[...] * jnp.dot(m_ref[...], wo_ref[...], preferred_element_type=F32)


def _resid(cfg, mixed, wo, x_p, x_s, ada3):
    tm, d = cfg.tm_proj, cfg.d_model
    vmem = (2 * d * d * 2 + 2 * tm * d * 2 + 7 * tm * d * 4) / MIB + 8
    return pl.pallas_call(
        functools.partial(_resid_kernel, n_p=cfg.tokens_p // tm), grid=(cfg.tokens // tm,),
        in_specs=[pl.BlockSpec((tm, d), lambda i: (i, 0)), pl.BlockSpec((d, d), lambda i: (0, 0))]
        + _two_group_specs(cfg, tm, d) + [_ada_spec(cfg, tm, 2)],
        out_specs=pl.BlockSpec((tm, d), lambda i: (i, 0)),
        out_shape=jax.ShapeDtypeStruct((cfg.tokens, d), F32),
        compiler_params=_params(vmem), name="resid")(mixed, wo, x_p, x_s, ada3)


def _router_kernel(x_ref, g_ref, sh_ref, sc_ref, wrh_ref, wrl_ref, rb_ref, tri_ref,
                   hp_ref, idx_ref, wgt_ref, pos_ref, cnt_ref, carry_ref, *, cfg):
    d, ne, ng = cfg.d_model, cfg.n_experts, cfg.n_egroups
    per = ne // ng
    i = pl.program_id(0)

    @pl.when(i == 0)
    def _():
        carry_ref[...] = jnp.zeros_like(carry_ref)

    h = _modulated_norm(x_ref[...], g_ref[...], sh_ref[...], sc_ref[...])
    tm = h.shape[0]
    _store_rows(hp_ref, 0, _pack_bf16_pair(h[:, :d // 2], h[:, d // 2:]))

    h_hi = h.astype(BF16)
    h_lo = (h - h_hi.astype(F32)).astype(BF16)
    nt_dot = lambda a, b: lax.dot_general(a, b, (((1,), (1,)), ((), ())), preferred_element_type=F32)
    logits = nt_dot(wrh_ref[...], h_hi) + (nt_dot(wrh_ref[...], h_lo) + nt_dot(wrl_ref[...], h_hi))
    scores = jax.nn.sigmoid(logits)
    sel = scores + rb_ref[...]
    s3 = sel.reshape(ng, per, tm)
    io3 = lax.broadcasted_iota(I32, (ng, per, tm), 1)
    m1 = jnp.max(s3, axis=1, keepdims=True)
    i1 = jnp.min(jnp.where(s3 == m1, io3, per), axis=1, keepdims=True)
    m2 = jnp.max(jnp.where(io3 == i1, -jnp.inf, s3), axis=1, keepdims=True)
    gs = (m1 + m2).reshape(ng, tm)
    gio = lax.broadcasted_iota(I32, (ng, tm), 0)
    before = jnp.zeros((ng, tm), I32)
    for g2 in range(ng):
        row = gs[g2:g2 + 1, :]
        ahead = jnp.where(row > gs, 1, jnp.where(row == gs, jnp.where(gio > g2, 1, 0), 0))
        before = before + ahead
    keep = jnp.where(before < cfg.topk_groups, 1.0, 0.0)
    keep_e = jnp.broadcast_to(keep.reshape(ng, 1, tm), (ng, per, tm)).reshape(ne, tm)
    cur = jnp.where(keep_e > 0.5, sel, -jnp.inf)

    eio = lax.broadcasted_iota(I32, (ne, tm), 0)
    chosen, picked = [], jnp.zeros((ne, tm), F32)
    for _ in range(cfg.top_k):
        m = jnp.max(cur, axis=0, keepdims=True)
        ij = jnp.min(jnp.where(cur == m, eio, ne), axis=0, keepdims=True)
        hit = eio == ij
        picked = jnp.where(hit, 1.0, picked)
        cur = jnp.where(hit, -jnp.inf, cur)
        chosen.append(ij)
    before_t = jnp.dot(picked.astype(BF16), tri_ref[...], preferred_element_type=F32)
    rank = carry_ref[...] + before_t
    sc_rows, pos_rows = [], []
    for ij in chosen:
        hit = eio == ij
        sc_rows.append(jnp.sum(jnp.where(hit, scores, 0.0), axis=0, keepdims=True))
        pos_rows.append(jnp.sum(jnp.where(hit, rank, 0.0), axis=0, keepdims=True))
    sc_all = jnp.concatenate(sc_rows, axis=0)
    total = jnp.sum(sc_all, axis=0, keepdims=True)
    idx_ref[...] = jnp.concatenate(chosen, axis=0)
    wgt_ref[...] = sc_all / total * cfg.routed_scale
    pos_ref[...] = jnp.concatenate(pos_rows, axis=0).astype(I32)
    carry_ref[...] = carry_ref[...] + jnp.sum(picked, axis=1, keepdims=True)
    cnt_ref[...] = carry_ref[...].astype(I32)


def _router(cfg, x1, g, ada3, w_router, router_bias):
    tm, d, ne, k = cfg.tm_router, cfg.d_model, cfg.n_experts, cfg.top_k
    t = cfg.tokens
    tri = jnp.asarray(np.triu(np.ones((tm, tm), np.float32), 1), BF16)
    row = lambda dt: jax.ShapeDtypeStruct((k, t), dt)
    row_spec = pl.BlockSpec((k, tm), lambda i: (0, i))
    wrt = w_router.T.astype(F32)
    wrt_hi = wrt.astype(BF16)
    wrt_lo = (wrt - wrt_hi.astype(F32)).astype(BF16)
    return pl.pallas_call(
        functools.partial(_router_kernel, cfg=cfg), grid=(t // tm,),
        in_specs=[pl.BlockSpec((tm, d), lambda i: (i, 0)), pl.BlockSpec((1, d), lambda i: (0, 0)),
                  _ada_spec(cfg, tm, 3), _ada_spec(cfg, tm, 4),
                  pl.BlockSpec((ne, d), lambda i: (0, 0)), pl.BlockSpec((ne, d), lambda i: (0, 0)),
                  pl.BlockSpec((ne, 1), lambda i: (0, 0)), pl.BlockSpec((tm, tm), lambda i: (0, 0))],
        out_specs=[pl.BlockSpec((tm * _lines(d // 2), LANES), lambda i: (i, 0)), row_spec, row_spec, row_spec,
                   pl.BlockSpec((ne, 1), lambda i: (0, 0))],
        out_shape=[jax.ShapeDtypeStruct((t * _lines(d // 2), LANES), U32), row(I32), row(F32), row(I32),
                   jax.ShapeDtypeStruct((ne, 1), I32)],
        scratch_shapes=[pltpu.VMEM((ne, 1), F32)],
        compiler_params=_params(40), name="router")(
            x1, g, ada3, ada3, wrt_hi, wrt_lo, router_bias.reshape(ne, 1).astype(F32), tri)


def _slots_kernel(pstart_ref, idx_ref, pos_ref, o_ref, *, n_experts):
    idx = idx_ref[...]
    base = lax.fori_loop(0, n_experts, lambda e, acc: jnp.where(idx == e, pstart_ref[e], acc),
                         jnp.zeros(idx.shape, I32))
    o_ref[...] = base + pos_ref[...]


def _slots(cfg, pstart, idx, pos):
    k, t = idx.shape
    tn = math.gcd(t, 4096)
    spec = pl.BlockSpec((k, tn), lambda i, *_: (0, i))
    grid_spec = pltpu.PrefetchScalarGridSpec(num_scalar_prefetch=1, grid=(t // tn,), in_specs=[spec, spec],
                                             out_specs=spec)
    return pl.pallas_call(functools.partial(_slots_kernel, n_experts=cfg.n_experts), grid_spec=grid_spec,
                          out_shape=jax.ShapeDtypeStruct((k, t), I32), name="slots")(pstart, idx, pos)


def _dispatch_kernel(pend_ref, padded_ref, nt_ref, hp_ref, slot_ref, xs_ref, zero_ref, sem, zsem, *, cfg):
    tm, tme, k, ne = cfg.tm_dispatch, cfg.tm_expert, cfg.top_k, cfg.n_experts
    lines = _lines(cfg.d_model // 2)
    n_tiles = xs_ref.shape[0] // (tme * lines)
    i = pl.program_id(0)

    def pad_copy(e):
        return pltpu.make_async_copy(zero_ref, xs_ref.at[_row_window(pend_ref[e] - tme, tme, lines)], zsem)

    def tail_copy(t):
        return pltpu.make_async_copy(zero_ref, xs_ref.at[_row_window(t * tme, tme, lines)], zsem)

    @pl.when(i == 0)
    def _():
        zero_ref[...] = jnp.zeros_like(zero_ref)

        def start(e, c):
            pl.when(padded_ref[e] > 0)(lambda: pad_copy(e).start())
            return c

        def wait(e, c):
            pl.when(padded_ref[e] > 0)(lambda: pad_copy(e).wait())
            return c

        lax.fori_loop(0, ne, start, 0)
        lax.fori_loop(nt_ref[0], n_tiles, lambda t, c: (tail_copy(t).start(), c)[1], 0)
        lax.fori_loop(0, ne, wait, 0)
        lax.fori_loop(nt_ref[0], n_tiles, lambda t, c: (tail_copy(t).wait(), c)[1], 0)

    def row(r, c):
        for j in range(k):
            pltpu.make_async_copy(hp_ref.at[_row_window(r, 1, lines)],
                                  xs_ref.at[_row_window(slot_ref[j, r], 1, lines)], sem).start(priority=j % 2)
        return c

    lax.fori_loop(0, tm, row, 0)
    for _ in range(k):
        pltpu.make_async_copy(hp_ref, xs_ref.at[_row_window(0, tm, lines)], sem).wait()


def _dispatch(cfg, hp, slot, pend, padded, n_used, n_slots):
    tm, d, k = cfg.tm_dispatch, cfg.d_model, cfg.top_k
    lines = _lines(d // 2)
    grid_spec = pltpu.PrefetchScalarGridSpec(
        num_scalar_prefetch=3, grid=(cfg.tokens // tm,),
        in_specs=[pl.BlockSpec((tm * lines, LANES), lambda i, *_: (i, 0)),
                  pl.BlockSpec((k, tm), lambda i, *_: (0, i), memory_space=pltpu.SMEM)],
        out_specs=pl.BlockSpec(memory_space=pl.ANY),
        scratch_shapes=[pltpu.VMEM((cfg.tm_expert * lines, LANES), U32), pltpu.SemaphoreType.DMA,
                        pltpu.SemaphoreType.DMA])
    return pl.pallas_call(
        functools.partial(_dispatch_kernel, cfg=cfg), grid_spec=grid_spec,
        out_shape=jax.ShapeDtypeStruct((n_slots * lines, LANES), U32),
        compiler_params=_params(24), name="dispatch")(pend, padded, n_used, hp, slot)


def _expert_kernel(te_ref, nt_ref, nxt_ref, xs_ref, wg_hbm, wu_hbm, wd_hbm, ys_ref,
                   wgf, wuf, wdf, wgb, wub, wdb, side_ref, wsem):
    i = pl.program_id(0)
    nt = nt_ref[0]
    tme = xs_ref.shape[0] // _lines(wdb.shape[1] // 2)
    half = wdb.shape[1] // 2

    def weight_copies(e, s):
        return [pltpu.make_async_copy(src.at[e], dst.at[s], wsem.at[s])
                for src, dst in ((wg_hbm, wgf), (wu_hbm, wuf), (wd_hbm, wdf))]

    @pl.when(i >= nt)
    def _():
        ys_ref[...] = jnp.zeros_like(ys_ref)

    @pl.when(i < nt)
    def _():
        e = te_ref[i]

        @pl.when(i == 0)
        def _():
            side_ref[0] = 0
            for c in weight_copies(e, 0):
                c.start()

        @pl.when(jnp.logical_or(i == 0, e != te_ref[jnp.maximum(i - 1, 0)]))
        def _():
            s = side_ref[0]
            nxt = nxt_ref[e]
            @pl.when(nxt >= 0)
            def _():
                for c in weight_copies(nxt, 1 - s):
                    c.start()

            for c in weight_copies(e, s):
                c.wait()
            wgb[...] = wgf[s].astype(BF16)
            wub[...] = wuf[s].astype(BF16)
            wdb[...] = wdf[s].astype(BF16)
            side_ref[0] = 1 - s

        lo, hi = _unpack_bf16_pair(_load_rows(xs_ref, 0, tme, half))
        x = jnp.concatenate([lo.astype(BF16), hi.astype(BF16)], axis=1)
        g = jnp.dot(x, wgb[...], preferred_element_type=F32)
        u = jnp.dot(x, wub[...], preferred_element_type=F32)
        a = (g * jax.nn.sigmoid(g) * u).astype(BF16)
        cb = _pair_block(2 * half)
        lines = _lines(half)
        for m in range(half // cb):
            ym = jnp.dot(a, wdb[:, 2 * m * cb:(2 * m + 2) * cb], preferred_element_type=F32)
            packed = _pack_bf16_pair(ym[:, :cb], ym[:, cb:])
            for c in range(cb // LANES):
                ys_ref[pl.ds(m * (cb // LANES) + c, tme, stride=lines), :] = packed[:, c * LANES:(c + 1) * LANES]


def _experts(cfg, xs, tile_expert, n_used, next_expert, wg, wu, wd):
    tme, d, de = cfg.tm_expert, cfg.d_model, cfg.d_expert
    lines = _lines(d // 2)
    n_tiles = xs.shape[0] // (tme * lines)
    any_spec = pl.BlockSpec(memory_space=pl.ANY)
    grid_spec = pltpu.PrefetchScalarGridSpec(
        num_scalar_prefetch=3, grid=(n_tiles,),
        in_specs=[pl.BlockSpec((tme * lines, LANES), lambda i, te, nt, nx: (jnp.minimum(i, nt[0] - 1), 0)),
                  any_spec, any_spec, any_spec],
        out_specs=pl.BlockSpec((tme * lines, LANES), lambda i, te, nt, nx: (i, 0)),
        scratch_shapes=[pltpu.VMEM((2, d, de), F32), pltpu.VMEM((2, d, de), F32), pltpu.VMEM((2, de, d), F32),
                        pltpu.VMEM((d, de), BF16), pltpu.VMEM((d, de), BF16), pltpu.VMEM((de, d), BF16),
                        pltpu.SMEM((1,), I32), pltpu.SemaphoreType.DMA((2,))])
    vmem = (2 * 3 * d * de * 4 + 3 * d * de * 2 + 6 * tme * d * 2 + 6 * tme * d * 4) / MIB + 8
    return pl.pallas_call(
        _expert_kernel, grid_spec=grid_spec, out_shape=jax.ShapeDtypeStruct(xs.shape, U32),
        compiler_params=_params(vmem), name="experts")(tile_expert, n_used, next_expert, xs, wg, wu, wd)


def _final_kernel(slot_ref, slotn_ref, x_ref, wt_ref, g_ref, sh_ref, sc_ref, gate_ref, wg_ref, wu_ref, wd_ref,
                  nf_ref, ys_ref, o_ref, buf0, buf1, sem, *, cfg):
    tm, k = cfg.tm_final, cfg.top_k
    half = cfg.d_model // 2
    lines = _lines(half)
    bufs = (buf0, buf1)
    i = pl.program_id(0)
    n = pl.num_programs(0)

    def row_copy(slots, b, j, r):
        return pltpu.make_async_copy(ys_ref.at[_row_window(slots[j, r], 1, lines)],
                                     bufs[b].at[_row_window(j * tm + r, 1, lines)], sem.at[b])

    def wait_planes(b):
        for j in range(k):
            pltpu.make_async_copy(ys_ref.at[_row_window(0, tm, lines)],
                                  bufs[b].at[_row_window(j * tm, tm, lines)], sem.at[b]).wait()

    @pl.when(i == 0)
    def _():
        def row(r, c):
            for j in range(k):
                row_copy(slot_ref, 0, j, r).start(priority=j % 2)
            return c

        lax.fori_loop(0, tm, row, 0)

    def step(b):
        wait_planes(b)
        for r in range(tm):
            for j in range(k):
                row_copy(slotn_ref, 1 - b, j, r).start(priority=j % 2)

        x = x_ref[...]
        h = _modulated_norm(x, g_ref[...], sh_ref[...], sc_ref[...]).astype(BF16)
        g = jnp.dot(h, wg_ref[...], preferred_element_type=F32)
        u = jnp.dot(h, wu_ref[...], preferred_element_type=F32)
        y = jnp.dot((g * jax.nn.sigmoid(g) * u).astype(BF16), wd_ref[...], preferred_element_type=F32)
        cb = _pair_block(2 * half)
        for j in range(k):
            lo, hi = _unpack_bf16_pair(_load_rows(bufs[b], j * tm, tm, half))
            blocks = [part[:, m * cb:(m + 1) * cb] for m in range(half // cb) for part in (lo, hi)]
            y = y + wt_ref[:, j:j + 1] * jnp.concatenate(blocks, axis=1)
        xo = x + gate_ref[...] * y
        o_ref[...] = xo * lax.rsqrt(jnp.mean(xo * xo, axis=-1, keepdims=True) + EPS) * nf_ref[...]
        pl.when(i == n - 1)(lambda: wait_planes(1 - b))

    for b in range(2):
        pl.when(i % 2 == b)(functools.partial(step, b))


def _final_group(cfg, tile_off, batch_off, nbatch, seq, x1, ys, slot, wgt_t, ada3, g, wgs, wus, wds, nf):
    tm, d, k, ds = cfg.tm_final, cfg.d_model, cfg.top_k, cfg.d_shared
    n = nbatch * seq // tm
    per_batch = seq // tm
    ada = lambda which: pl.BlockSpec((None, 1, d), lambda i: ((batch_off + i // per_batch) * 6 + which, 0, 0))
    const = lambda shape: pl.BlockSpec(shape, lambda i: (0, 0))
    vmem = (2 * k * tm * d * 2 + 2 * 3 * d * ds * 2 + 10 * tm * d * 4) / MIB + 8
    return pl.pallas_call(
        functools.partial(_final_kernel, cfg=cfg), grid=(n,),
        in_specs=[pl.BlockSpec((k, tm), lambda i: (0, tile_off + i), memory_space=pltpu.SMEM),
                  pl.BlockSpec((k, tm), lambda i: (0, tile_off + jnp.minimum(i + 1, n - 1)), memory_space=pltpu.SMEM),
                  pl.BlockSpec((tm, d), lambda i: (tile_off + i, 0)),
                  pl.BlockSpec((tm, k), lambda i: (tile_off + i, 0)),
                  const((1, d)), ada(3), ada(4), ada(5), const((d, ds)), const((d, ds)), const((ds, d)),
                  const((1, d)), pl.BlockSpec(memory_space=pl.ANY)],
        out_specs=pl.BlockSpec((tm, d), lambda i: (i, 0)),
        out_shape=jax.ShapeDtypeStruct((nbatch * seq, d), F32),
        scratch_shapes=[pltpu.VMEM((k * tm * _lines(d // 2), LANES), U32),
                        pltpu.VMEM((k * tm * _lines(d // 2), LANES), U32), pltpu.SemaphoreType.DMA((2,))],
        compiler_params=_params(vmem), name="final")(
            slot, slot, x1, wgt_t, g, ada3, ada3, ada3, wgs, wus, wds, nf, ys)


def _forward(cfg, x_prompt, x_sample, c_prompt, c_sample, w_ada, b_ada, norm_mix, w_in, rel_bias, sink,
             w_attn_proj, w_four, b_four, w_out, norm_ffn, w_router, router_bias, w_gate_e, w_up_e, w_down_e,
             w_gate_s, w_up_s, w_down_s, norm_final):
    d, t = cfg.d_model, cfg.tokens
    aw, kvw, fw = cfg.attn_width, cfg.kv_width, cfg.fourier_width
    x_p = x_prompt.reshape(cfg.tokens_p, d)
    x_s = x_sample.reshape(cfg.tokens_s, d)

    nc = cfg.batch + cfg.dec_batch
    rows = -(-nc // 8) * 8
    c_all = jnp.concatenate([c_prompt, c_sample, jnp.zeros((rows - nc, d), F32)], axis=0)
    ada3 = _ada(cfg, c_all, w_ada[0], b_ada).reshape(rows * 6, 1, d)

    h = _modnorm(cfg, x_p, x_s, norm_mix, ada3)
    w_in_b = w_in[0].astype(BF16)
    c0, c1 = aw + 2 * kvw, aw + 2 * kvw + fw
    qkv = _matmul(h, w_in_b[:, :c0], cfg.tm_mm, c0, "mm_qkv")
    u = _matmul(h, w_in_b[:, c0:c1], cfg.tm_mm, fw, "mm_u")
    gates = _matmul(h, w_in_b[:, c1:], cfg.tm_mm, d, "mm_gates")

    attn = _attention(cfg, qkv, _bias_table(cfg, rel_bias), sink[0])
    u2 = u.reshape(t // cfg.four_n2, cfg.four_n2 * fw)
    f_p = _fourier_group(cfg, u2, 0, cfg.batch, cfg.seq)
    f_s = _fourier_group(cfg, u2, cfg.tokens_p // cfg.four_n2, cfg.dec_batch, cfg.dec_seq)
    mixed = _proj(cfg, attn, f_p, f_s, gates, w_attn_proj[0].astype(BF16), w_four[0].astype(BF16), b_four)
    x1 = _resid(cfg, mixed, w_out[0].astype(BF16), x_p, x_s, ada3)

    hp, idx, wgt, pos, cnt = _router(cfg, x1, norm_ffn, ada3, w_router[0], router_bias[0])
    tme, ne = cfg.tm_expert, cfg.n_experts
    counts = cnt[:, 0]
    padded = (counts + tme - 1) // tme * tme
    pend = jnp.cumsum(padded).astype(I32)
    slot = _slots(cfg, pend - padded, idx, pos)
    n_tiles = t * cfg.top_k // tme + ne
    tile_start = jnp.arange(n_tiles, dtype=I32) * tme
    tile_expert = jnp.minimum(jnp.sum((pend[None, :] <= tile_start[:, None]).astype(I32), axis=1), ne - 1)
    n_used = (pend[-1:] // tme).astype(I32)
    owner = jnp.where(counts > 0, jnp.arange(ne, dtype=I32), ne)
    later = jnp.concatenate([lax.cummin(owner, reverse=True)[1:], jnp.full((1,), ne, I32)])
    next_expert = jnp.where(later < ne, later, -1).astype(I32)
    xs = _dispatch(cfg, hp, slot, pend, padded, n_used, n_tiles * tme)
    ys = _experts(cfg, xs, tile_expert, n_used, next_expert, w_gate_e[0], w_up_e[0], w_down_e[0])

    wgt_t = wgt.T
    shared = (w_gate_s[0].astype(BF16), w_up_s[0].astype(BF16), w_down_s[0].astype(BF16))
    nf = norm_final.reshape(1, d)
    y_p = _final_group(cfg, 0, 0, cfg.batch, cfg.seq, x1, ys, slot, wgt_t, ada3, norm_ffn, *shared, nf)
    y_s = _final_group(cfg, cfg.tokens_p // cfg.tm_final, cfg.batch, cfg.dec_batch, cfg.dec_seq, x1, ys, slot,
                       wgt_t, ada3, norm_ffn, *shared, nf)
    return (y_p.reshape(cfg.batch, cfg.seq, d), y_s.reshape(cfg.dec_batch, cfg.dec_seq, d))


def kernel(x_prompt, x_sample, c_prompt, c_sample, w_ada, b_ada, norm_mix, w_in, rel_bias, sink, w_attn_proj,
           w_four, b_four, w_out, norm_ffn, w_router, router_bias, w_gate_e, w_up_e, w_down_e, w_gate_s, w_up_s,
           w_down_s, norm_final):
    return _forward(Cfg(), x_prompt, x_sample, c_prompt, c_sample, w_ada, b_ada, norm_mix, w_in, rel_bias, sink,
                    w_attn_proj, w_four, b_four, w_out, norm_ffn, w_router, router_bias, w_gate_e, w_up_e,
                    w_down_e, w_gate_s, w_up_s, w_down_s, norm_final)
```

```python
import functools
import math
from typing import NamedTuple

import jax
import jax.numpy as jnp
import numpy as np
from jax import lax
from jax.experimental import pallas as pl
from jax.experimental.pallas import tpu as pltpu

F32 = jnp.float32
BF16 = jnp.bfloat16
I32 = jnp.int32
U32 = jnp.uint32

EPS = 1e-6
NEG_INF = -1e30
LANES = 128
MIB = 1024 * 1024


class Cfg(NamedTuple):
    d_model: int = 2048
    batch: int = 4
    seq: int = 8192
    dec_batch: int = 8
    dec_seq: int = 2048
    n_heads: int = 32
    n_kv: int = 4
    head_dim: int = 64
    window: int = 128
    n_buckets: int = 32
    max_distance: int = 128
    n_fgroups: int = 8
    fgroup: int = 128
    n_experts: int = 256
    top_k: int = 8
    n_egroups: int = 8
    topk_groups: int = 4
    d_expert: int = 512
    d_shared: int = 512
    routed_scale: float = 2.5
    tm_norm: int = 512
    tm_mm: int = 512
    tm_proj: int = 256
    tm_router: int = 256
    tm_dispatch: int = 256
    tm_expert: int = 256
    tm_final: int = 256
    four_n2: int = 128
    four_t2: int = 8

    @property
    def attn_width(self):
        return self.n_heads * self.head_dim

    @property
    def kv_width(self):
        return self.n_kv * self.head_dim

    @property
    def fourier_width(self):
        return self.n_fgroups * self.fgroup

    @property
    def tokens_p(self):
        return self.batch * self.seq

    @property
    def tokens_s(self):
        return self.dec_batch * self.dec_seq

    @property
    def tokens(self):
        return self.tokens_p + self.tokens_s


def _params(vmem_mib):
    return pltpu.CompilerParams(vmem_limit_bytes=int(vmem_mib * MIB))


def _batch_of_tile(i, cfg, tm):
    n_p = cfg.tokens_p // tm
    return jnp.where(i < n_p, i // (cfg.seq // tm), cfg.batch + (i - n_p) // (cfg.dec_seq // tm))


def _ada_spec(cfg, tm, which):
    return pl.BlockSpec((None, 1, cfg.d_model), lambda i: (_batch_of_tile(i, cfg, tm) * 6 + which, 0, 0))


def _two_group_specs(cfg, tm, width):
    n_p = cfg.tokens_p // tm
    return [pl.BlockSpec((tm, width), lambda i: (jnp.minimum(i, n_p - 1), 0)),
            pl.BlockSpec((tm, width), lambda i: (jnp.maximum(i - n_p, 0), 0))]


def _modulated_norm(x, g, shift, scale):
    y = x * lax.rsqrt(jnp.mean(x * x, axis=-1, keepdims=True) + EPS) * g
    return y * (1.0 + scale) + shift


def _pack_bf16_pair(lo, hi):
    lo_bits = pltpu.bitcast(lo.astype(BF16).astype(F32), U32) >> 16
    hi_bits = pltpu.bitcast(hi.astype(BF16).astype(F32), U32) & jnp.uint32(0xFFFF0000)
    return hi_bits | lo_bits


def _unpack_bf16_pair(p):
    lo = pltpu.bitcast(p << 16, F32)
    hi = pltpu.bitcast(p & jnp.uint32(0xFFFF0000), F32)
    return lo, hi


def _pair_block(d):
    return min(256, d // 2)


def _lines(width):
    return width // LANES


def _row_window(row, n_rows, lines):
    return pl.ds(pl.multiple_of(row * lines, lines), n_rows * lines)


def _load_rows(ref, row, n_rows, width):
    lines = _lines(width)
    base = pl.multiple_of(row * lines, lines) if not isinstance(row, int) else row * lines
    return jnp.concatenate([ref[pl.ds(base + c, n_rows, stride=lines), :] for c in range(lines)], axis=1)


def _store_rows(ref, row, mat):
    n_rows, width = mat.shape
    lines = _lines(width)
    for c in range(lines):
        ref[pl.ds(row * lines + c, n_rows, stride=lines), :] = mat[:, c * LANES:(c + 1) * LANES]


def _ada_kernel(c_ref, w_ref, b_ref, o_ref):
    c = c_ref[...]
    s = c * jax.nn.sigmoid(c)
    o_ref[...] = jnp.dot(s, w_ref[...], preferred_element_type=F32, precision=lax.Precision.HIGHEST) + b_ref[...]


def _ada(cfg, c_all, w_ada, b_ada):
    rows, d = c_all.shape
    n = w_ada.shape[1]
    tn = n // 8
    return pl.pallas_call(
        _ada_kernel, grid=(n // tn,),
        in_specs=[pl.BlockSpec((rows, d), lambda j: (0, 0)), pl.BlockSpec((d, tn), lambda j: (0, j)),
                  pl.BlockSpec((1, tn), lambda j: (0, j))],
        out_specs=pl.BlockSpec((rows, tn), lambda j: (0, j)),
        out_shape=jax.ShapeDtypeStruct((rows, n), F32),
        compiler_params=_params(2 * d * tn * 4 / MIB + 8), name="ada")(c_all, w_ada, b_ada)


def _modnorm_kernel(xp_ref, xs_ref, g_ref, sh_ref, sc_ref, o_ref, *, n_p):
    i = pl.program_id(0)

    def run(x_ref):
        o_ref[...] = _modulated_norm(x_ref[...], g_ref[...], sh_ref[...], sc_ref[...]).astype(o_ref.dtype)

    pl.when(i < n_p)(lambda: run(xp_ref))
    pl.when(i >= n_p)(lambda: run(xs_ref))


def _modnorm(cfg, x_p, x_s, g, ada3):
    tm, d = cfg.tm_norm, cfg.d_model
    return pl.pallas_call(
        functools.partial(_modnorm_kernel, n_p=cfg.tokens_p // tm), grid=(cfg.tokens // tm,),
        in_specs=_two_group_specs(cfg, tm, d) + [pl.BlockSpec((1, d), lambda i: (0, 0)),
                                                 _ada_spec(cfg, tm, 0), _ada_spec(cfg, tm, 1)],
        out_specs=pl.BlockSpec((tm, d), lambda i: (i, 0)),
        out_shape=jax.ShapeDtypeStruct((cfg.tokens, d), BF16),
        compiler_params=_params(7 * tm * d * 4 / MIB + 8), name="modnorm")(x_p, x_s, g, ada3, ada3)


def _mm_kernel(a_ref, w_ref, o_ref):
    o_ref[...] = jnp.dot(a_ref[...], w_ref[...], preferred_element_type=F32).astype(o_ref.dtype)


def _matmul(a, w, tm, tn, name):
    m, k = a.shape
    n = w.shape[1]
    vmem = (2 * tm * k * 2 + 2 * k * tn * 2 + 2 * tm * tn * 2 + tm * tn * 4) / MIB + 8
    return pl.pallas_call(
        _mm_kernel, grid=(n // tn, m // tm),
        in_specs=[pl.BlockSpec((tm, k), lambda j, i: (i, 0)), pl.BlockSpec((k, tn), lambda j, i: (0, j))],
        out_specs=pl.BlockSpec((tm, tn), lambda j, i: (i, j)),
        out_shape=jax.ShapeDtypeStruct((m, n), BF16),
        compiler_params=_params(vmem), name=name)(a, w)


def _t5_bucket_table(cfg):
    blk, span = cfg.window, 3 * cfg.window
    rel = (np.arange(span)[None, :] - cfg.window - np.arange(blk)[:, None]).astype(np.int32)
    nb = cfg.n_buckets // 2
    max_exact = nb // 2
    ret = np.where(rel > 0, nb, 0)
    n = np.abs(rel)
    nf = np.maximum(n, 1).astype(np.float32)
    ratio = np.log(nf / np.float32(max_exact)) / np.float32(math.log(cfg.max_distance / max_exact))
    large = max_exact + (ratio * np.float32(nb - max_exact)).astype(np.int32)
    large = np.minimum(large, nb - 1)
    return (ret + np.where(n < max_exact, n, large)).astype(np.int32), rel


def _bias_kernel(bucket_ref, rbt_ref, o_ref, *, n_buckets):
    bucket = bucket_ref[...]
    acc = jnp.zeros(o_ref.shape, F32)
    for b in range(n_buckets):
        acc = jnp.where(bucket == b, rbt_ref[:, b:b + 1], acc)
    o_ref[...] = acc


def _bias_table(cfg, rel_bias):
    buckets, _ = _t5_bucket_table(cfg)
    blk, span, h = cfg.window, 3 * cfg.window, cfg.n_heads
    n = blk * span
    flat = pl.pallas_call(
        functools.partial(_bias_kernel, n_buckets=cfg.n_buckets),
        out_shape=jax.ShapeDtypeStruct((h, n), F32), name="bias")(
            jnp.asarray(buckets.reshape(1, n)), rel_bias.T.astype(F32))
    pairs = h // cfg.n_kv // 2
    t = flat.reshape(cfg.n_kv, pairs, 2, blk, span)
    return jnp.transpose(t, (0, 1, 3, 2, 4)).reshape(cfg.n_kv, pairs * blk, 2 * span)


def _attn_kernel(q_ref, kp_ref, kc_ref, kn_ref, vp_ref, vc_ref, vn_ref, bias_ref, sink_ref, rel_ref, kofs_ref,
                 ones_ref, o_ref, valid_ref, s_ref, l_ref, p_ref, m_ref, *, cfg, blocks_p):
    blk, hd = cfg.window, cfg.head_dim
    span = 3 * blk
    pairs = cfg.n_heads // cfg.n_kv // 2
    rc = 64
    nb_p, nb_s = cfg.seq // blk, cfg.dec_seq // blk
    i = pl.program_id(0)
    in_p = i < blocks_p
    pos = jnp.where(in_p, i % nb_p, (i - blocks_p) % nb_s)
    seq_len = jnp.where(in_p, cfg.seq, cfg.dec_seq)
    at_edge = jnp.logical_or(pos == 0, (pos + 1) * blk == seq_len)

    @pl.when(at_edge)
    def _():
        kpos = pos * blk + kofs_ref[...]
        in_band = jnp.abs(rel_ref[...]) <= cfg.window
        in_seq = (kpos >= 0) & (kpos < seq_len)
        valid_ref[...] = jnp.where(in_band, jnp.where(in_seq, 1.0, 0.0), 0.0)

    @pl.when(jnp.logical_not(at_edge))
    def _():
        valid_ref[...] = jnp.where(jnp.abs(rel_ref[...]) <= cfg.window, 1.0, 0.0)

    k = jnp.concatenate([kp_ref[...], kc_ref[...], kn_ref[...]], axis=0)
    v = jnp.concatenate([vp_ref[...], vc_ref[...], vn_ref[...]], axis=0)
    scale = hd ** -0.5
    lane = lax.broadcasted_iota(I32, (pairs * blk, 2 * hd), 1)
    for j in range(cfg.n_kv):
        kj = k[:, j * hd:(j + 1) * hd]
        vj = v[:, j * hd:(j + 1) * hd]
        zero = jnp.zeros_like(kj)
        k2 = jnp.concatenate([jnp.concatenate([kj, zero], axis=1), jnp.concatenate([zero, kj], axis=1)], axis=0)
        v2 = jnp.concatenate([jnp.concatenate([vj, zero], axis=1), jnp.concatenate([zero, vj], axis=1)], axis=0)
        base = j * pairs * 2 * hd
        qs = jnp.concatenate([q_ref[:, base + p * 2 * hd: base + (p + 1) * 2 * hd] for p in range(pairs)], axis=0)
        s_ref[...] = lax.dot_general(qs, k2, (((1,), (1,)), ((), ())), preferred_element_type=F32)
        units = [(slice(r0, r0 + rc), half) for r0 in range(0, pairs * blk, rc) for half in range(2)]

        for rows, half in units:
            cols = slice(half * span, (half + 1) * span)
            lh = s_ref[rows, cols] * scale + bias_ref[j, rows, cols]
            lh = jnp.where(valid_ref[rows, cols] > 0.5, lh, NEG_INF)
            l_ref[rows, cols] = lh
            sk = sink_ref[j, rows, half * hd:half * hd + 1]
            m = jnp.maximum(jnp.max(lh, axis=-1, keepdims=True), sk)
            m_ref[rows, half * LANES:(half + 1) * LANES] = jnp.broadcast_to(m, (rc, LANES))
        for rows, half in units:
            cols = slice(half * span, (half + 1) * span)
            mb = m_ref[rows, half * LANES:(half + 1) * LANES]
            p = jnp.exp(l_ref[rows, cols] - jnp.concatenate([mb] * (span // LANES), axis=1))
            p_ref[rows, cols] = p.astype(BF16)
        probs = p_ref[...]
        pv = jnp.dot(probs, v2, preferred_element_type=F32)
        psum = jnp.dot(probs, ones_ref[...], preferred_element_type=F32)
        m_lane = jnp.where(lane < hd, m_ref[:, :LANES], m_ref[:, LANES:])
        out = pv / (psum + jnp.exp(sink_ref[j] - m_lane))
        for p in range(pairs):
            o_ref[:, base + p * 2 * hd: base + (p + 1) * 2 * hd] = out[p * blk:(p + 1) * blk].astype(o_ref.dtype)


def _attention(cfg, qkv, bias_tbl, sink):
    blk, hd = cfg.window, cfg.head_dim
    aw, kvw = cfg.attn_width, cfg.kv_width
    pairs = cfg.n_heads // cfg.n_kv // 2
    nb_p, nb_s = cfg.seq // blk, cfg.dec_seq // blk
    blocks_p = cfg.batch * nb_p
    n_blocks = cfg.tokens // blk
    _, rel = _t5_bucket_table(cfg)
    rel_tbl = np.tile(rel, (pairs, 2)).astype(np.int32)
    kofs_tbl = np.tile((np.arange(3 * blk) - cfg.window)[None, :], (pairs * blk, 2)).astype(np.int32)
    sink_tbl = jnp.broadcast_to(sink.astype(F32).reshape(cfg.n_kv, pairs, 1, 2, 1),
                                (cfg.n_kv, pairs, blk, 2, hd)).reshape(cfg.n_kv, pairs * blk, 2 * hd)
    ones_tbl = jnp.asarray(np.kron(np.eye(2), np.ones((3 * blk, hd))), BF16)

    def seq_pos(i):
        in_p = i < blocks_p
        return jnp.where(in_p, i % nb_p, (i - blocks_p) % nb_s), jnp.where(in_p, nb_p, nb_s)

    def prev_blk(i):
        pos, _ = seq_pos(i)
        return jnp.where(pos == 0, i, i - 1)

    def next_blk(i):
        pos, nb = seq_pos(i)
        return jnp.where(pos == nb - 1, i, i + 1)

    kcol, vcol = aw // kvw, aw // kvw + 1
    full = lambda shape: pl.BlockSpec(shape, lambda i: (0,) * len(shape))
    return pl.pallas_call(
        functools.partial(_attn_kernel, cfg=cfg, blocks_p=blocks_p), grid=(n_blocks,),
        in_specs=[pl.BlockSpec((blk, aw), lambda i: (i, 0)),
                  pl.BlockSpec((blk, kvw), lambda i: (prev_blk(i), kcol)),
                  pl.BlockSpec((blk, kvw), lambda i: (i, kcol)),
                  pl.BlockSpec((blk, kvw), lambda i: (next_blk(i), kcol)),
                  pl.BlockSpec((blk, kvw), lambda i: (prev_blk(i), vcol)),
                  pl.BlockSpec((blk, kvw), lambda i: (i, vcol)),
                  pl.BlockSpec((blk, kvw), lambda i: (next_blk(i), vcol)),
                  full(bias_tbl.shape), full(sink_tbl.shape), full(rel_tbl.shape), full(kofs_tbl.shape),
                  full(ones_tbl.shape)],
        out_specs=pl.BlockSpec((blk, aw), lambda i: (i, 0)),
        out_shape=jax.ShapeDtypeStruct((cfg.tokens, aw), BF16),
        scratch_shapes=[pltpu.VMEM(rel_tbl.shape, F32), pltpu.VMEM(rel_tbl.shape, F32), pltpu.VMEM(rel_tbl.shape, F32),
                        pltpu.VMEM(rel_tbl.shape, BF16), pltpu.VMEM((pairs * blk, 2 * LANES), F32)],
        compiler_params=_params(40), name="attn")(
            qkv, qkv, qkv, qkv, qkv, qkv, qkv, bias_tbl, sink_tbl, jnp.asarray(rel_tbl), jnp.asarray(kofs_tbl),
            ones_tbl)


def _dft_cos_sin(n):
    ang = 2.0 * np.pi * np.outer(np.arange(n), np.arange(n)) / n
    return np.cos(ang), np.sin(ang)


def _four1_kernel(u_ref, m1_ref, twr_ref, twi_ref, zr_ref, zi_ref, *, n1, t2, width):
    y = jnp.dot(m1_ref[...], u_ref[...], preferred_element_type=F32)
    twr, twi = twr_ref[0], twi_ref[0]
    for l in range(t2):
        sl = slice(l * width, (l + 1) * width)
        yr, yi = y[:n1, sl], y[n1:, sl]
        cr, ci = twr[:, l:l + 1], twi[:, l:l + 1]
        zr_ref[:, sl] = (yr * cr - yi * ci).astype(zr_ref.dtype)
        zi_ref[:, sl] = (yr * ci + yi * cr).astype(zi_ref.dtype)


def _four3_kernel(zr_ref, zi_ref, m3_ref, mc_ref, o_ref, *, n2, n_groups, group, norm):
    z = jnp.concatenate([zr_ref[0], zi_ref[0]], axis=0)
    x = jnp.dot(m3_ref[...], z, preferred_element_type=F32)
    xr, xi = x[:n2].astype(BF16), x[n2:].astype(BF16)
    for g in range(n_groups):
        sl = slice(g * group, (g + 1) * group)
        xg = jnp.concatenate([xr[:, sl], xi[:, sl]], axis=1)
        o_ref[:, sl] = (jnp.dot(xg, mc_ref[...], preferred_element_type=F32) * norm).astype(o_ref.dtype)


def _fourier_group(cfg, u2, row_off, nbatch, seq):
    n2, t2, width = cfg.four_n2, cfg.four_t2, cfg.fourier_width
    n1 = seq // n2
    c1, s1 = _dft_cos_sin(n1)
    m1 = jnp.asarray(np.concatenate([c1, -s1], axis=0), BF16)
    ang = 2.0 * np.pi * np.outer(np.arange(n1), np.arange(n2)) / seq
    twr = jnp.asarray(np.cos(ang).reshape(n1, n2 // t2, t2).transpose(1, 0, 2), F32)
    twi = jnp.asarray((-np.sin(ang)).reshape(n1, n2 // t2, t2).transpose(1, 0, 2), F32)
    blk_off = row_off // n1
    zshape = jax.ShapeDtypeStruct((nbatch * n1, n2 * width), BF16)
    tw_spec = pl.BlockSpec((1, n1, t2), lambda b, j: (j, 0, 0))
    zr, zi = pl.pallas_call(
        functools.partial(_four1_kernel, n1=n1, t2=t2, width=width), grid=(nbatch, n2 // t2),
        in_specs=[pl.BlockSpec((n1, t2 * width), lambda b, j: (blk_off + b, j)),
                  pl.BlockSpec((2 * n1, n1), lambda b, j: (0, 0)), tw_spec, tw_spec],
        out_specs=[pl.BlockSpec((n1, t2 * width), lambda b, j: (b, j))] * 2,
        out_shape=[zshape, zshape],
        compiler_params=_params(32), name="four1")(u2, m1, twr, twi)

    c2, s2 = _dft_cos_sin(n2)
    m3 = jnp.asarray(np.block([[c2, s2], [-s2, c2]]), BF16)
    cc, sc = _dft_cos_sin(cfg.fgroup)
    mc = jnp.asarray(np.concatenate([cc, sc], axis=0), BF16)
    norm = 1.0 / math.sqrt(seq * cfg.fgroup)
    z_spec = pl.BlockSpec((1, n2, width), lambda b, k1: (b * n1 + k1, 0, 0))
    f = pl.pallas_call(
        functools.partial(_four3_kernel, n2=n2, n_groups=cfg.n_fgroups, group=cfg.fgroup, norm=norm),
        grid=(nbatch, n1),
        in_specs=[z_spec, z_spec, pl.BlockSpec((2 * n2, 2 * n2), lambda b, k1: (0, 0)),
                  pl.BlockSpec((2 * cfg.fgroup, cfg.fgroup), lambda b, k1: (0, 0))],
        out_specs=pl.BlockSpec((n2, width), lambda b, k1: (b, k1)),
        out_shape=jax.ShapeDtypeStruct((nbatch * n2, n1 * width), BF16),
        compiler_params=_params(32), name="four3")(
            zr.reshape(nbatch * n1, n2, width), zi.reshape(nbatch * n1, n2, width), m3, mc)
    return f.reshape(nbatch * seq, width)


def _proj_kernel(a_ref, fp_ref, fs_ref, g_ref, wa_ref, wf_ref, bf_ref, o_ref, *, n_p, d):
    i = pl.program_id(0)
    fm = jnp.where(i < n_p, fp_ref[...], fs_ref[...])
    a = jnp.dot(a_ref[...], wa_ref[...], preferred_element_type=F32)
    f = jnp.dot(fm, wf_ref[...], preferred_element_type=F32) + bf_ref[...]
    ga = g_ref[:, :d].astype(F32)
    gf = g_ref[:, d:].astype(F32)
    o_ref[...] = (jax.nn.sigmoid(ga) * a + jax.nn.sigmoid(gf) * f).astype(o_ref.dtype)


def _proj(cfg, attn, f_p, f_s, gates, wa, wf, bf):
    tm, d, aw, fw = cfg.tm_proj, cfg.d_model, cfg.attn_width, cfg.fourier_width
    const = lambda shape: pl.BlockSpec(shape, lambda i: (0, 0))
    vmem = (2 * (aw + fw) * d * 2 + 2 * tm * (aw + 2 * fw + 3 * d) * 2 + 3 * tm * d * 4) / MIB + 8
    return pl.pallas_call(
        functools.partial(_proj_kernel, n_p=cfg.tokens_p // tm, d=d), grid=(cfg.tokens // tm,),
        in_specs=[pl.BlockSpec((tm, aw), lambda i: (i, 0))] + _two_group_specs(cfg, tm, fw) + [
            pl.BlockSpec((tm, 2 * d), lambda i: (i, 0)), const((aw, d)), const((fw, d)), const((1, d))],
        out_specs=pl.BlockSpec((tm, d), lambda i: (i, 0)),
        out_shape=jax.ShapeDtypeStruct((cfg.tokens, d), BF16),
        compiler_params=_params(vmem), name="proj")(attn, f_p, f_s, gates, wa, wf, bf)


def _resid_kernel(m_ref, wo_ref, xp_ref, xs_ref, gate_ref, o_ref, *, n_p):
    i = pl.program_id(0)
    x = jnp.where(i < n_p, xp_ref[...], xs_ref[...])
    o_ref[...] = x + gate_ref[...] * jnp.dot(m_ref[...], wo_ref[...], preferred_element_type=F32)


def _resid(cfg, mixed, wo, x_p, x_s, ada3):
    tm, d = cfg.tm_proj, cfg.d_model
    vmem = (2 * d * d * 2 + 2 * tm * d * 2 + 7 * tm * d * 4) / MIB + 8
    return pl.pallas_call(
        functools.partial(_resid_kernel, n_p=cfg.tokens_p // tm), grid=(cfg.tokens // tm,),
        in_specs=[pl.BlockSpec((tm, d), lambda i: (i, 0)), pl.BlockSpec((d, d), lambda i: (0, 0))]
        + _two_group_specs(cfg, tm, d) + [_ada_spec(cfg, tm, 2)],
        out_specs=pl.BlockSpec((tm, d), lambda i: (i, 0)),
        out_shape=jax.ShapeDtypeStruct((cfg.tokens, d), F32),
        compiler_params=_params(vmem), name="resid")(mixed, wo, x_p, x_s, ada3)


def _router_kernel(x_ref, g_ref, sh_ref, sc_ref, wrh_ref, wrl_ref, rb_ref, tri_ref,
                   hp_ref, idx_ref, wgt_ref, pos_ref, cnt_ref, carry_ref, *, cfg):
    d, ne, ng = cfg.d_model, cfg.n_experts, cfg.n_egroups
    per = ne // ng
    i = pl.program_id(0)

    @pl.when(i == 0)
    def _():
        carry_ref[...] = jnp.zeros_like(carry_ref)

    h = _modulated_norm(x_ref[...], g_ref[...], sh_ref[...], sc_ref[...])
    tm = h.shape[0]
    _store_rows(hp_ref, 0, _pack_bf16_pair(h[:, :d // 2], h[:, d // 2:]))

    h_hi = h.astype(BF16)
    h_lo = (h - h_hi.astype(F32)).astype(BF16)
    nt_dot = lambda a, b: lax.dot_general(a, b, (((1,), (1,)), ((), ())), preferred_element_type=F32)
    logits = nt_dot(wrh_ref[...], h_hi) + (nt_dot(wrh_ref[...], h_lo) + nt_dot(wrl_ref[...], h_hi))
    scores = jax.nn.sigmoid(logits)
    sel = scores + rb_ref[...]
    s3 = sel.reshape(ng, per, tm)
    io3 = lax.broadcasted_iota(I32, (ng, per, tm), 1)
    m1 = jnp.max(s3, axis=1, keepdims=True)
    i1 = jnp.min(jnp.where(s3 == m1, io3, per), axis=1, keepdims=True)
    m2 = jnp.max(jnp.where(io3 == i1, -jnp.inf, s3), axis=1, keepdims=True)
    gs = (m1 + m2).reshape(ng, tm)
    gio = lax.broadcasted_iota(I32, (ng, tm), 0)
    before = jnp.zeros((ng, tm), I32)
    for g2 in range(ng):
        row = gs[g2:g2 + 1, :]
        ahead = jnp.where(row > gs, 1, jnp.where(row == gs, jnp.where(gio > g2, 1, 0), 0))
        before = before + ahead
    keep = jnp.where(before < cfg.topk_groups, 1.0, 0.0)
    keep_e = jnp.broadcast_to(keep.reshape(ng, 1, tm), (ng, per, tm)).reshape(ne, tm)
    cur = jnp.where(keep_e > 0.5, sel, -jnp.inf)

    eio = lax.broadcasted_iota(I32, (ne, tm), 0)
    chosen, picked = [], jnp.zeros((ne, tm), F32)
    for _ in range(cfg.top_k):
        m = jnp.max(cur, axis=0, keepdims=True)
        ij = jnp.min(jnp.where(cur == m, eio, ne), axis=0, keepdims=True)
        hit = eio == ij
        picked = jnp.where(hit, 1.0, picked)
        cur = jnp.where(hit, -jnp.inf, cur)
        chosen.append(ij)
    before_t = jnp.dot(picked.astype(BF16), tri_ref[...], preferred_element_type=F32)
    rank = carry_ref[...] + before_t
    sc_rows, pos_rows = [], []
    for ij in chosen:
        hit = eio == ij
        sc_rows.append(jnp.sum(jnp.where(hit, scores, 0.0), axis=0, keepdims=True))
        pos_rows.append(jnp.sum(jnp.where(hit, rank, 0.0), axis=0, keepdims=True))
    sc_all = jnp.concatenate(sc_rows, axis=0)
    total = jnp.sum(sc_all, axis=0, keepdims=True)
    idx_ref[...] = jnp.concatenate(chosen, axis=0)
    wgt_ref[...] = sc_all / total * cfg.routed_scale
    pos_ref[...] = jnp.concatenate(pos_rows, axis=0).astype(I32)
    carry_ref[...] = carry_ref[...] + jnp.sum(picked, axis=1, keepdims=True)
    cnt_ref[...] = carry_ref[...].astype(I32)


def _router(cfg, x1, g, ada3, w_router, router_bias):
    tm, d, ne, k = cfg.tm_router, cfg.d_model, cfg.n_experts, cfg.top_k
    t = cfg.tokens
    tri = jnp.asarray(np.triu(np.ones((tm, tm), np.float32), 1), BF16)
    row = lambda dt: jax.ShapeDtypeStruct((k, t), dt)
    row_spec = pl.BlockSpec((k, tm), lambda i: (0, i))
    wrt = w_router.T.astype(F32)
    wrt_hi = wrt.astype(BF16)
    wrt_lo = (wrt - wrt_hi.astype(F32)).astype(BF16)
    return pl.pallas_call(
        functools.partial(_router_kernel, cfg=cfg), grid=(t // tm,),
        in_specs=[pl.BlockSpec((tm, d), lambda i: (i, 0)), pl.BlockSpec((1, d), lambda i: (0, 0)),
                  _ada_spec(cfg, tm, 3), _ada_spec(cfg, tm, 4),
                  pl.BlockSpec((ne, d), lambda i: (0, 0)), pl.BlockSpec((ne, d), lambda i: (0, 0)),
                  pl.BlockSpec((ne, 1), lambda i: (0, 0)), pl.BlockSpec((tm, tm), lambda i: (0, 0))],
        out_specs=[pl.BlockSpec((tm * _lines(d // 2), LANES), lambda i: (i, 0)), row_spec, row_spec, row_spec,
                   pl.BlockSpec((ne, 1), lambda i: (0, 0))],
        out_shape=[jax.ShapeDtypeStruct((t * _lines(d // 2), LANES), U32), row(I32), row(F32), row(I32),
                   jax.ShapeDtypeStruct((ne, 1), I32)],
        scratch_shapes=[pltpu.VMEM((ne, 1), F32)],
        compiler_params=_params(40), name="router")(
            x1, g, ada3, ada3, wrt_hi, wrt_lo, router_bias.reshape(ne, 1).astype(F32), tri)


def _slots_kernel(pstart_ref, idx_ref, pos_ref, o_ref, *, n_experts):
    idx = idx_ref[...]
    base = lax.fori_loop(0, n_experts, lambda e, acc: jnp.where(idx == e, pstart_ref[e], acc),
                         jnp.zeros(idx.shape, I32))
    o_ref[...] = base + pos_ref[...]


def _slots(cfg, pstart, idx, pos):
    k, t = idx.shape
    tn = math.gcd(t, 4096)
    spec = pl.BlockSpec((k, tn), lambda i, *_: (0, i))
    grid_spec = pltpu.PrefetchScalarGridSpec(num_scalar_prefetch=1, grid=(t // tn,), in_specs=[spec, spec],
                                             out_specs=spec)
    return pl.pallas_call(functools.partial(_slots_kernel, n_experts=cfg.n_experts), grid_spec=grid_spec,
                          out_shape=jax.ShapeDtypeStruct((k, t), I32), name="slots")(pstart, idx, pos)


def _dispatch_kernel(pend_ref, padded_ref, nt_ref, hp_ref, slot_ref, xs_ref, zero_ref, sem, zsem, *, cfg):
    tm, tme, k, ne = cfg.tm_dispatch, cfg.tm_expert, cfg.top_k, cfg.n_experts
    lines = _lines(cfg.d_model // 2)
    n_tiles = xs_ref.shape[0] // (tme * lines)
    i = pl.program_id(0)

    def pad_copy(e):
        return pltpu.make_async_copy(zero_ref, xs_ref.at[_row_window(pend_ref[e] - tme, tme, lines)], zsem)

    def tail_copy(t):
        return pltpu.make_async_copy(zero_ref, xs_ref.at[_row_window(t * tme, tme, lines)], zsem)

    @pl.when(i == 0)
    def _():
        zero_ref[...] = jnp.zeros_like(zero_ref)

        def start(e, c):
            pl.when(padded_ref[e] > 0)(lambda: pad_copy(e).start())
            return c

        def wait(e, c):
            pl.when(padded_ref[e] > 0)(lambda: pad_copy(e).wait())
            return c

        lax.fori_loop(0, ne, start, 0)
        lax.fori_loop(nt_ref[0], n_tiles, lambda t, c: (tail_copy(t).start(), c)[1], 0)
        lax.fori_loop(0, ne, wait, 0)
        lax.fori_loop(nt_ref[0], n_tiles, lambda t, c: (tail_copy(t).wait(), c)[1], 0)

    def row(r, c):
        for j in range(k):
            pltpu.make_async_copy(hp_ref.at[_row_window(r, 1, lines)],
                                  xs_ref.at[_row_window(slot_ref[j, r], 1, lines)], sem).start(priority=j % 2)
        return c

    lax.fori_loop(0, tm, row, 0)
    for _ in range(k):
        pltpu.make_async_copy(hp_ref, xs_ref.at[_row_window(0, tm, lines)], sem).wait()


def _dispatch(cfg, hp, slot, pend, padded, n_used, n_slots):
    tm, d, k = cfg.tm_dispatch, cfg.d_model, cfg.top_k
    lines = _lines(d // 2)
    grid_spec = pltpu.PrefetchScalarGridSpec(
        num_scalar_prefetch=3, grid=(cfg.tokens // tm,),
        in_specs=[pl.BlockSpec((tm * lines, LANES), lambda i, *_: (i, 0)),
                  pl.BlockSpec((k, tm), lambda i, *_: (0, i), memory_space=pltpu.SMEM)],
        out_specs=pl.BlockSpec(memory_space=pl.ANY),
        scratch_shapes=[pltpu.VMEM((cfg.tm_expert * lines, LANES), U32), pltpu.SemaphoreType.DMA,
                        pltpu.SemaphoreType.DMA])
    return pl.pallas_call(
        functools.partial(_dispatch_kernel, cfg=cfg), grid_spec=grid_spec,
        out_shape=jax.ShapeDtypeStruct((n_slots * lines, LANES), U32),
        compiler_params=_params(24), name="dispatch")(pend, padded, n_used, hp, slot)


def _expert_kernel(te_ref, nt_ref, nxt_ref, xs_hbm, wg_hbm, wu_hbm, wd_hbm, ys_ref,
                   wgf, wuf, wdf, wgb, wub, wdb, xring, side_ref, wsem, xsem):
    i = pl.program_id(0)
    nt = nt_ref[0]
    half = wdb.shape[1] // 2
    lines = _lines(half)
    tme = ys_ref.shape[0] // lines
    depth = xring.shape[0]

    def x_copy(t):
        return pltpu.make_async_copy(xs_hbm.at[_row_window(t * tme, tme, lines)], xring.at[t % depth],
                                     xsem.at[t % depth])

    @pl.when(i == 0)
    def _():
        for t in range(depth - 1):
            pl.when(t < nt)(lambda t=t: x_copy(t).start())

    def weight_copies(e, s):
        return [pltpu.make_async_copy(src.at[e], dst.at[s], wsem.at[s])
                for src, dst in ((wg_hbm, wgf), (wu_hbm, wuf), (wd_hbm, wdf))]

    @pl.when(i >= nt)
    def _():
        ys_ref[...] = jnp.zeros_like(ys_ref)

    @pl.when(i < nt)
    def _():
        e = te_ref[i]

        @pl.when(i == 0)
        def _():
            side_ref[0] = 0
            for c in weight_copies(e, 0):
                c.start()

        @pl.when(jnp.logical_or(i == 0, e != te_ref[jnp.maximum(i - 1, 0)]))
        def _():
            s = side_ref[0]
            nxt = nxt_ref[e]
            @pl.when(nxt >= 0)
            def _():
                for c in weight_copies(nxt, 1 - s):
                    c.start()

            for c in weight_copies(e, s):
                c.wait()
            wgb[...] = wgf[s].astype(BF16)
            wub[...] = wuf[s].astype(BF16)
            wdb[...] = wdf[s].astype(BF16)
            side_ref[0] = 1 - s

        pl.when(i + depth - 1 < nt)(lambda: x_copy(i + depth - 1).start())
        x_copy(i).wait()
        lo, hi = _unpack_bf16_pair(_load_rows(xring.at[i % depth], 0, tme, half))
        x = jnp.concatenate([lo.astype(BF16), hi.astype(BF16)], axis=1)
        g = jnp.dot(x, wgb[...], preferred_element_type=F32)
        u = jnp.dot(x, wub[...], preferred_element_type=F32)
        a = (g * jax.nn.sigmoid(g) * u).astype(BF16)
        cb = _pair_block(2 * half)
        lines = _lines(half)
        for m in range(half // cb):
            ym = jnp.dot(a, wdb[:, 2 * m * cb:(2 * m + 2) * cb], preferred_element_type=F32)
            packed = _pack_bf16_pair(ym[:, :cb], ym[:, cb:])
            for c in range(cb // LANES):
                ys_ref[pl.ds(m * (cb // LANES) + c, tme, stride=lines), :] = packed[:, c * LANES:(c + 1) * LANES]


def _experts(cfg, xs, tile_expert, n_used, next_expert, wg, wu, wd):
    tme, d, de = cfg.tm_expert, cfg.d_model, cfg.d_expert
    lines = _lines(d // 2)
    n_tiles = xs.shape[0] // (tme * lines)
    any_spec = pl.BlockSpec(memory_space=pl.ANY)
    grid_spec = pltpu.PrefetchScalarGridSpec(
        num_scalar_prefetch=3, grid=(n_tiles,),
        in_specs=[any_spec, any_spec, any_spec, any_spec],
        out_specs=pl.BlockSpec((tme * lines, LANES), lambda i, te, nt, nx: (i, 0)),
        scratch_shapes=[pltpu.VMEM((2, d, de), F32), pltpu.VMEM((2, d, de), F32), pltpu.VMEM((2, de, d), F32),
                        pltpu.VMEM((d, de), BF16), pltpu.VMEM((d, de), BF16), pltpu.VMEM((de, d), BF16),
                        pltpu.VMEM((3, tme * lines, LANES), U32),
                        pltpu.SMEM((1,), I32), pltpu.SemaphoreType.DMA((2,)), pltpu.SemaphoreType.DMA((3,))])
    vmem = (2 * 3 * d * de * 4 + 3 * d * de * 2 + 6 * tme * d * 2 + 6 * tme * d * 4) / MIB + 8
    return pl.pallas_call(
        _expert_kernel, grid_spec=grid_spec, out_shape=jax.ShapeDtypeStruct(xs.shape, U32),
        compiler_params=_params(vmem), name="experts")(tile_expert, n_used, next_expert, xs, wg, wu, wd)


def _final_kernel(slot_ref, slotn_ref, x_ref, wt_ref, g_ref, sh_ref, sc_ref, gate_ref, wg_ref, wu_ref, wd_ref,
                  nf_ref, ys_ref, o_ref, buf0, buf1, sem, *, cfg):
    tm, k = cfg.tm_final, cfg.top_k
    half = cfg.d_model // 2
    lines = _lines(half)
    bufs = (buf0, buf1)
    i = pl.program_id(0)
    n = pl.num_programs(0)

    def row_copy(slots, b, j, r):
        return pltpu.make_async_copy(ys_ref.at[_row_window(slots[j, r], 1, lines)],
                                     bufs[b].at[_row_window(j * tm + r, 1, lines)], sem.at[b])

    def wait_planes(b):
        for j in range(k):
            pltpu.make_async_copy(ys_ref.at[_row_window(0, tm, lines)],
                                  bufs[b].at[_row_window(j * tm, tm, lines)], sem.at[b]).wait()

    @pl.when(i == 0)
    def _():
        def row(r, c):
            for j in range(k):
                row_copy(slot_ref, 0, j, r).start(priority=j % 2)
            return c

        lax.fori_loop(0, tm, row, 0)

    def step(b):
        wait_planes(b)
        for r in range(tm):
            for j in range(k):
                row_copy(slotn_ref, 1 - b, j, r).start(priority=j % 2)

        x = x_ref[...]
        h = _modulated_norm(x, g_ref[...], sh_ref[...], sc_ref[...]).astype(BF16)
        g = jnp.dot(h, wg_ref[...], preferred_element_type=F32)
        u = jnp.dot(h, wu_ref[...], preferred_element_type=F32)
        y = jnp.dot((g * jax.nn.sigmoid(g) * u).astype(BF16), wd_ref[...], preferred_element_type=F32)
        cb = _pair_block(2 * half)
        for j in range(k):
            lo, hi = _unpack_bf16_pair(_load_rows(bufs[b], j * tm, tm, half))
            blocks = [part[:, m * cb:(m + 1) * cb] for m in range(half // cb) for part in (lo, hi)]
            y = y + wt_ref[:, j:j + 1] * jnp.concatenate(blocks, axis=1)
        xo = x + gate_ref[...] * y
        o_ref[...] = xo * lax.rsqrt(jnp.mean(xo * xo, axis=-1, keepdims=True) + EPS) * nf_ref[...]
        pl.when(i == n - 1)(lambda: wait_planes(1 - b))

    for b in range(2):
        pl.when(i % 2 == b)(functools.partial(step, b))


def _final_group(cfg, tile_off, batch_off, nbatch, seq, x1, ys, slot, wgt_t, ada3, g, wgs, wus, wds, nf):
    tm, d, k, ds = cfg.tm_final, cfg.d_model, cfg.top_k, cfg.d_shared
    n = nbatch * seq // tm
    per_batch = seq // tm
    ada = lambda which: pl.BlockSpec((None, 1, d), lambda i: ((batch_off + i // per_batch) * 6 + which, 0, 0))
    const = lambda shape: pl.BlockSpec(shape, lambda i: (0, 0))
    vmem = (2 * k * tm * d * 2 + 2 * 3 * d * ds * 2 + 10 * tm * d * 4) / MIB + 8
    return pl.pallas_call(
        functools.partial(_final_kernel, cfg=cfg), grid=(n,),
        in_specs=[pl.BlockSpec((k, tm), lambda i: (0, tile_off + i), memory_space=pltpu.SMEM),
                  pl.BlockSpec((k, tm), lambda i: (0, tile_off + jnp.minimum(i + 1, n - 1)), memory_space=pltpu.SMEM),
                  pl.BlockSpec((tm, d), lambda i: (tile_off + i, 0)),
                  pl.BlockSpec((tm, k), lambda i: (tile_off + i, 0)),
                  const((1, d)), ada(3), ada(4), ada(5), const((d, ds)), const((d, ds)), const((ds, d)),
                  const((1, d)), pl.BlockSpec(memory_space=pl.ANY)],
        out_specs=pl.BlockSpec((tm, d), lambda i: (i, 0)),
        out_shape=jax.ShapeDtypeStruct((nbatch * seq, d), F32),
        scratch_shapes=[pltpu.VMEM((k * tm * _lines(d // 2), LANES), U32),
                        pltpu.VMEM((k * tm * _lines(d // 2), LANES), U32), pltpu.SemaphoreType.DMA((2,))],
        compiler_params=_params(vmem), name="final")(
            slot, slot, x1, wgt_t, g, ada3, ada3, ada3, wgs, wus, wds, nf, ys)


def _forward(cfg, x_prompt, x_sample, c_prompt, c_sample, w_ada, b_ada, norm_mix, w_in, rel_bias, sink,
             w_attn_proj, w_four, b_four, w_out, norm_ffn, w_router, router_bias, w_gate_e, w_up_e, w_down_e,
             w_gate_s, w_up_s, w_down_s, norm_final):
    d, t = cfg.d_model, cfg.tokens
    aw, kvw, fw = cfg.attn_width, cfg.kv_width, cfg.fourier_width
    x_p = x_prompt.reshape(cfg.tokens_p, d)
    x_s = x_sample.reshape(cfg.tokens_s, d)

    nc = cfg.batch + cfg.dec_batch
    rows = -(-nc // 8) * 8
    c_all = jnp.concatenate([c_prompt, c_sample, jnp.zeros((rows - nc, d), F32)], axis=0)
    ada3 = _ada(cfg, c_all, w_ada[0], b_ada).reshape(rows * 6, 1, d)

    h = _modnorm(cfg, x_p, x_s, norm_mix, ada3)
    w_in_b = w_in[0].astype(BF16)
    c0, c1 = aw + 2 * kvw, aw + 2 * kvw + fw
    qkv = _matmul(h, w_in_b[:, :c0], cfg.tm_mm, c0, "mm_qkv")
    u = _matmul(h, w_in_b[:, c0:c1], cfg.tm_mm, fw, "mm_u")
    gates = _matmul(h, w_in_b[:, c1:], cfg.tm_mm, d, "mm_gates")

    attn = _attention(cfg, qkv, _bias_table(cfg, rel_bias), sink[0])
    u2 = u.reshape(t // cfg.four_n2, cfg.four_n2 * fw)
    f_p = _fourier_group(cfg, u2, 0, cfg.batch, cfg.seq)
    f_s = _fourier_group(cfg, u2, cfg.tokens_p // cfg.four_n2, cfg.dec_batch, cfg.dec_seq)
    mixed = _proj(cfg, attn, f_p, f_s, gates, w_attn_proj[0].astype(BF16), w_four[0].astype(BF16), b_four)
    x1 = _resid(cfg, mixed, w_out[0].astype(BF16), x_p, x_s, ada3)

    hp, idx, wgt, pos, cnt = _router(cfg, x1, norm_ffn, ada3, w_router[0], router_bias[0])
    tme, ne = cfg.tm_expert, cfg.n_experts
    counts = cnt[:, 0]
    padded = (counts + tme - 1) // tme * tme
    pend = jnp.cumsum(padded).astype(I32)
    slot = _slots(cfg, pend - padded, idx, pos)
    n_tiles = t * cfg.top_k // tme + ne
    tile_start = jnp.arange(n_tiles, dtype=I32) * tme
    tile_expert = jnp.minimum(jnp.sum((pend[None, :] <= tile_start[:, None]).astype(I32), axis=1), ne - 1)
    n_used = (pend[-1:] // tme).astype(I32)
    owner = jnp.where(counts > 0, jnp.arange(ne, dtype=I32), ne)
    later = jnp.concatenate([lax.cummin(owner, reverse=True)[1:], jnp.full((1,), ne, I32)])
    next_expert = jnp.where(later < ne, later, -1).astype(I32)
    xs = _dispatch(cfg, hp, slot, pend, padded, n_used, n_tiles * tme)
    ys = _experts(cfg, xs, tile_expert, n_used, next_expert, w_gate_e[0], w_up_e[0], w_down_e[0])

    wgt_t = wgt.T
    shared = (w_gate_s[0].astype(BF16), w_up_s[0].astype(BF16), w_down_s[0].astype(BF16))
    nf = norm_final.reshape(1, d)
    y_p = _final_group(cfg, 0, 0, cfg.batch, cfg.seq, x1, ys, slot, wgt_t, ada3, norm_ffn, *shared, nf)
    y_s = _final_group(cfg, cfg.tokens_p // cfg.tm_final, cfg.batch, cfg.dec_batch, cfg.dec_seq, x1, ys, slot,
                       wgt_t, ada3, norm_ffn, *shared, nf)
    return (y_p.reshape(cfg.batch, cfg.seq, d), y_s.reshape(cfg.dec_batch, cfg.dec_seq, d))


def kernel(x_prompt, x_sample, c_prompt, c_sample, w_ada, b_ada, norm_mix, w_in, rel_bias, sink, w_attn_proj,
           w_four, b_four, w_out, norm_ffn, w_router, router_bias, w_gate_e, w_up_e, w_down_e, w_gate_s, w_up_s,
           w_down_s, norm_final):
    return _forward(Cfg(), x_prompt, x_sample, c_prompt, c_sample, w_ada, b_ada, norm_mix, w_in, rel_bias, sink,
                    w_attn_proj, w_four, b_four, w_out, norm_ffn, w_router, router_bias, w_gate_e, w_up_e,
                    w_down_e, w_gate_s, w_up_s, w_down_s, norm_final)
```
